```python
import math
import jax, jax.numpy as jnp
from jax import lax
import numpy as np

D_MODEL = 4096
BATCH = 4
SEQ = 2048
DEPTH = 2

CHUNK = 64
Q_BLOCK = 128
A_HEADS = 16
A_QK_DIM = 64
A_V_DIM = 128
B_HEADS = 16
B_HEAD_DIM = 128
IDX_HEADS = 16
IDX_DIM = 64
TOPK_MAX = 256
N_BUCKETS = 32
MAX_DISTANCE = 128
N_HEADS_TOTAL = A_HEADS + B_HEADS
D_FF = 11008
CONV_WIDTH = 3
EPS = 1e-6

A_Q = A_HEADS * 2 * A_QK_DIM
A_K = A_HEADS * 2 * A_QK_DIM
A_V = A_HEADS * A_V_DIM
B_Q = B_HEADS * B_HEAD_DIM
B_K = B_HEAD_DIM
B_V = B_HEAD_DIM
I_Q = IDX_HEADS * IDX_DIM
I_K = IDX_DIM
I_W = IDX_HEADS
IN_SPLITS = (A_Q, A_K, A_V, B_Q, B_K, B_V, I_Q, I_K, I_W)
D_IN = A_Q + A_K + A_V + B_Q + B_K + B_V + I_Q + I_K + I_W
D_MIX = A_HEADS * A_V_DIM + B_HEADS * B_HEAD_DIM

kernel_name = "hybrid_diffattn_dsa_convglu"


def rms_norm(x, g):
    xf = x.astype(jnp.float32)
    y = xf * lax.rsqrt(jnp.mean(xf * xf, axis=-1, keepdims=True) + EPS)
    return (y * g.astype(jnp.float32)).astype(x.dtype)


def rel_bucket(rel):
    nb = N_BUCKETS // 2
    max_exact = nb // 2
    bucket = jnp.where(rel > 0, nb, 0)
    n = jnp.abs(rel)
    nf = jnp.maximum(n, 1).astype(jnp.float32)
    large = max_exact + (jnp.log(nf / max_exact) / math.log(MAX_DISTANCE / max_exact)
                         * (nb - max_exact)).astype(jnp.int32)
    large = jnp.minimum(large, nb - 1)
    return bucket + jnp.where(n < max_exact, n, large)


def chunk_mask(tq, ts):
    return (ts[None, :] // CHUNK) <= (tq[:, None] // CHUNK)


def diff_attention(q, k, v, lam, rel_table):
    s_len = q.shape[1]
    scale = A_QK_DIM ** -0.5
    outs = []
    for i in range(s_len // Q_BLOCK):
        q0, end = i * Q_BLOCK, (i + 1) * Q_BLOCK
        tq = jnp.arange(q0, end)
        ts = jnp.arange(end)
        bias = rel_table[rel_bucket(ts[None, :] - tq[:, None])][..., :A_HEADS]
        bias = jnp.transpose(bias, (2, 0, 1))[None, :, None].astype(jnp.float32)
        logits = jnp.einsum('bqhjd,bshjd->bhjqs', q[:, q0:end], k[:, :end]).astype(jnp.float32) * scale + bias
        logits = jnp.where(chunk_mask(tq, ts), logits, -jnp.inf)
        p = jax.nn.softmax(logits, axis=-1)
        attn = p[:, :, 0] - lam * p[:, :, 1]
        outs.append(jnp.einsum('bhqs,bshd->bqhd', attn.astype(v.dtype), v[:, :end]))
    return jnp.concatenate(outs, axis=1)


def dsa_attention(q, k, v, qi, ki, wi, rel_table):
    s_len = q.shape[1]
    topk = min(TOPK_MAX, s_len // 4)
    scale = B_HEAD_DIM ** -0.5
    outs = []
    for i in range(s_len // Q_BLOCK):
        q0, end = i * Q_BLOCK, (i + 1) * Q_BLOCK
        tq = jnp.arange(q0, end)
        ts = jnp.arange(end)
        idx_logits = jnp.einsum('bqhd,bsd->bqhs', qi[:, q0:end], ki[:, :end]) * (IDX_DIM ** -0.5)
        score = jnp.einsum('bqhs,bqh->bqs', jax.nn.relu(idx_logits), wi[:, q0:end] * (IDX_HEADS ** -0.5))
        score = jnp.where(chunk_mask(tq, ts)[None], score.astype(jnp.float32), -jnp.inf)
        kk = min(topk, end)
        vals, sel = lax.top_k(score, kk)
        valid = jnp.isfinite(vals)
        kg = jax.vmap(lambda a, ix: a[ix])(k[:, :end], sel)
        vg = jax.vmap(lambda a, ix: a[ix])(v[:, :end], sel)
        bias = rel_table[rel_bucket(sel - tq[None, :, None])][..., A_HEADS:]
        bias = jnp.swapaxes(bias, 2, 3).astype(jnp.float32)
        logits = jnp.einsum('bqhd,bqkd->bqhk', q[:, q0:end], kg).astype(jnp.float32) * scale + bias
        logits = jnp.where(valid[:, :, None, :], logits, -jnp.inf)
        p = jax.nn.softmax(logits, axis=-1)
        outs.append(jnp.einsum('bqhk,bqkd->bqhd', p.astype(v.dtype), vg))
    return jnp.concatenate(outs, axis=1)


def causal_dwconv(h, w, b):
    y = lax.conv_general_dilated(
        h, w[:, None, :].astype(h.dtype), window_strides=(1,),
        padding=((CONV_WIDTH - 1, 0),), dimension_numbers=('NWC', 'WIO', 'NWC'),
        feature_group_count=h.shape[-1])
    return y + b


def setup_inputs(seed: int = 0) -> dict:
    key = jax.random.key(seed)
    ks = jax.random.split(key, 20)
    f32 = jnp.float32
    nrm = lambda k, shape, s: jax.random.normal(k, shape, f32) * s
    gain = lambda k, shape: 1.0 + 0.02 * jax.random.normal(k, shape, f32)
    return {
        "x": nrm(ks[0], (BATCH, SEQ, D_MODEL), 1.0),
        "attn_norm": gain(ks[1], (DEPTH, D_MODEL)),
        "w_in": nrm(ks[2], (DEPTH, D_MODEL, D_IN), D_MODEL ** -0.5),
        "a_q_norm": gain(ks[3], (DEPTH, A_QK_DIM)),
        "a_k_norm": gain(ks[4], (DEPTH, A_QK_DIM)),
        "lambda_qk": nrm(ks[5], (DEPTH, 4, A_QK_DIM), 0.1),
        "a_out_norm": gain(ks[6], (DEPTH, A_V_DIM)),
        "b_q_norm": gain(ks[7], (DEPTH, B_HEAD_DIM)),
        "b_k_norm": gain(ks[8], (DEPTH, B_HEAD_DIM)),
        "rel_bias": nrm(ks[9], (N_BUCKETS, N_HEADS_TOTAL), 0.5),
        "w_out": nrm(ks[10], (DEPTH, D_MIX, D_MODEL), D_MIX ** -0.5),
        "ffn_norm": gain(ks[11], (DEPTH, D_MODEL)),
        "w_gate_up": nrm(ks[12], (DEPTH, D_MODEL, 2 * D_FF), D_MODEL ** -0.5),
        "conv_w": nrm(ks[13], (DEPTH, CONV_WIDTH, D_FF), CONV_WIDTH ** -0.5),
        "conv_b": nrm(ks[14], (DEPTH, D_FF), 0.02),
        "w_down": nrm(ks[15], (DEPTH, D_FF, D_MODEL), D_FF ** -0.5),
    }


def reference(x, attn_norm, w_in, a_q_norm, a_k_norm, lambda_qk, a_out_norm,
              b_q_norm, b_k_norm, rel_bias, w_out, ffn_norm, w_gate_up, conv_w,
              conv_b, w_down):
    bsz, s_len, _ = x.shape
    offsets = np.cumsum(IN_SPLITS)[:-1].tolist()
    for l in range(DEPTH):
        h = rms_norm(x, attn_norm[l])
        proj = h @ w_in[l]
        qa, ka, va, qb, kb, vb, qi, ki, wi = jnp.split(proj, offsets, axis=-1)
        qa = rms_norm(qa.reshape(bsz, s_len, A_HEADS, 2, A_QK_DIM), a_q_norm[l])
        ka = rms_norm(ka.reshape(bsz, s_len, A_HEADS, 2, A_QK_DIM), a_k_norm[l])
        va = va.reshape(bsz, s_len, A_HEADS, A_V_DIM)
        lam_init = 0.8 - 0.6 * math.exp(-0.3 * l)
        lq = lambda_qk[l].astype(jnp.float32)
        lam = jnp.exp(jnp.sum(lq[0] * lq[1])) - jnp.exp(jnp.sum(lq[2] * lq[3])) + lam_init
        oa = diff_attention(qa, ka, va, lam, rel_bias)
        oa = rms_norm(oa, a_out_norm[l]) * (1.0 - lam_init)
        qb = rms_norm(qb.reshape(bsz, s_len, B_HEADS, B_HEAD_DIM), b_q_norm[l])
        kb = rms_norm(kb, b_k_norm[l])
        ob = dsa_attention(qb, kb, vb,
                           qi.reshape(bsz, s_len, IDX_HEADS, IDX_DIM), ki, wi, rel_bias)
        mix = jnp.concatenate([oa.reshape(bsz, s_len, -1), ob.reshape(bsz, s_len, -1)], axis=-1)
        x = x + mix @ w_out[l]
        h = rms_norm(x, ffn_norm[l])
        g, u = jnp.split(h @ w_gate_up[l], 2, axis=-1)
        g = causal_dwconv(g, conv_w[l], conv_b[l])
        x = x + (jax.nn.silu(g) * u) @ w_down[l]
    return x
```

```python
import functools
import math

import numpy as np
import jax
import jax.numpy as jnp
from jax import lax
from jax.experimental import pallas as pl
from jax.experimental.pallas import tpu as pltpu

CHUNK = 64
CHUNK_SHIFT = 6
A_HEADS = 16
A_QK_DIM = 64
A_V_DIM = 128
B_HEADS = 16
B_HEAD_DIM = 128
IDX_HEADS = 16
IDX_DIM = 64
TOPK_MAX = 256
N_BUCKETS = 32
MAX_DISTANCE = 128
CONV_WIDTH = 3
EPS = 1e-6

A_Q = A_HEADS * 2 * A_QK_DIM
A_K = A_HEADS * 2 * A_QK_DIM
A_V = A_HEADS * A_V_DIM
B_Q = B_HEADS * B_HEAD_DIM
B_K = B_HEAD_DIM
B_V = B_HEAD_DIM
I_Q = IDX_HEADS * IDX_DIM
I_K = IDX_DIM
I_W = IDX_HEADS
OFF_AQ = 0
OFF_AK = OFF_AQ + A_Q
OFF_AV = OFF_AK + A_K
OFF_BQ = OFF_AV + A_V
OFF_BK = OFF_BQ + B_Q
OFF_BV = OFF_BK + B_K
OFF_IQ = OFF_BV + B_V
OFF_IK = OFF_IQ + I_Q
OFF_IW = OFF_IK + I_K
D_IN = OFF_IW + I_W

LANES = 128
V7X_VMEM_LIMIT = 56 * 1024 * 1024

ATT_T = 128
IDX_KB = 256
D_IN_PAD = 9728
MXU_DTYPE = jnp.bfloat16

NEG_INF = float("-inf")
M_FLOOR = -1e30
INT_MIN = -(2 ** 31)


def _cparams(*sem):
    return pltpu.CompilerParams(dimension_semantics=sem, vmem_limit_bytes=V7X_VMEM_LIMIT)


def _pick(n, pref):
    t = min(pref, n)
    while n % t:
        t -= LANES
    return t


def _dot(a, b):
    return jnp.dot(a, b, preferred_element_type=jnp.float32)


def _dot_nt(a, b):
    return lax.dot_general(a, b, (((1,), (1,)), ((), ())), preferred_element_type=jnp.float32)


def _rmsnorm_kernel(x_ref, g_ref, o_ref):
    x = x_ref[...]
    ms = jnp.mean(x * x, axis=-1, keepdims=True)
    o_ref[...] = (x * lax.rsqrt(ms + EPS) * g_ref[...]).astype(o_ref.dtype)


def _rmsnorm(x2d, g):
    m, d = x2d.shape
    tm = _pick(m, 256)
    return pl.pallas_call(
        _rmsnorm_kernel,
        out_shape=jax.ShapeDtypeStruct((m, d), MXU_DTYPE),
        grid=(m // tm,),
        in_specs=[pl.BlockSpec((tm, d), lambda i: (i, 0)),
                  pl.BlockSpec((1, d), lambda i: (0, 0))],
        out_specs=pl.BlockSpec((tm, d), lambda i: (i, 0)),
        compiler_params=_cparams("parallel"),
        name="rmsnorm",
    )(x2d, g.reshape(1, d))


def _mm_kernel(a_ref, w_ref, o_ref):
    o_ref[...] = _dot(a_ref[...], w_ref[...]).astype(o_ref.dtype)


def _mm_res_kernel(a_ref, w_ref, r_ref, o_ref):
    o_ref[...] = (r_ref[...] + _dot(a_ref[...], w_ref[...])).astype(o_ref.dtype)


def _mm2_res_kernel(a1_ref, a2_ref, w1_ref, w2_ref, r_ref, o_ref):
    acc = _dot(a1_ref[...], w1_ref[...]) + _dot(a2_ref[...], w2_ref[...])
    o_ref[...] = (r_ref[...] + acc).astype(o_ref.dtype)


def _matmul2_res(a1, a2, w, res, *, tm_pref, tn_pref, name):
    m, k = a1.shape
    n = w.shape[1]
    tm, tn = _pick(m, tm_pref), _pick(n, tn_pref)
    return pl.pallas_call(
        _mm2_res_kernel,
        out_shape=jax.ShapeDtypeStruct((m, n), jnp.float32),
        grid=(m // tm, n // tn),
        in_specs=[pl.BlockSpec((tm, k), lambda i, j: (i, 0)),
                  pl.BlockSpec((tm, k), lambda i, j: (i, 0)),
                  pl.BlockSpec((k, tn), lambda i, j: (0, j)),
                  pl.BlockSpec((k, tn), lambda i, j: (1, j)),
                  pl.BlockSpec((tm, tn), lambda i, j: (i, j))],
        out_specs=pl.BlockSpec((tm, tn), lambda i, j: (i, j)),
        compiler_params=_cparams("parallel", "arbitrary"),
        name=name,
    )(a1, a2, w, w, res)


def _matmul(a, w, res=None, *, tm_pref, tn_pref, name):
    m, k = a.shape
    n = w.shape[1]
    tm, tn = _pick(m, tm_pref), _pick(n, tn_pref)
    in_specs = [pl.BlockSpec((tm, k), lambda i, j: (i, 0)),
                pl.BlockSpec((k, tn), lambda i, j: (0, j))]
    args = [a, w]
    kern = _mm_kernel
    if res is not None:
        in_specs.append(pl.BlockSpec((tm, tn), lambda i, j: (i, j)))
        args.append(res)
        kern = _mm_res_kernel
    return pl.pallas_call(
        kern,
        out_shape=jax.ShapeDtypeStruct((m, n), jnp.float32),
        grid=(m // tm, n // tn),
        in_specs=in_specs,
        out_specs=pl.BlockSpec((tm, tn), lambda i, j: (i, j)),
        compiler_params=_cparams("parallel", "arbitrary"),
        name=name,
    )(*args)


def _seg_ones(seg):
    r = lax.broadcasted_iota(jnp.int32, (LANES, LANES), 0) // seg
    c = lax.broadcasted_iota(jnp.int32, (LANES, LANES), 1) // seg
    return (r == c).astype(MXU_DTYPE)


def _seg_rms(x, ones, seg):
    sq = x * x
    hi = sq.astype(MXU_DTYPE)
    lo = (sq - hi.astype(jnp.float32)).astype(MXU_DTYPE)
    ss = _dot(hi, ones) + _dot(lo, ones)
    return x * lax.rsqrt(ss * (1.0 / seg) + EPS)


def _prep_kernel(p_ref, gaq_ref, gak_ref, gbq_ref, gbk_ref,
                 qa2_ref, ka_ref, va_ref, qb_ref, kb_ref, vb_ref, qi2_ref, ki2_ref):
    ones64 = _seg_ones(A_QK_DIM)
    ones128 = _seg_ones(B_HEAD_DIM)
    lane = lax.broadcasted_iota(jnp.int32, (1, LANES), 1)
    lo_half = lane < A_QK_DIM
    gaq, gak, gbq, gbk = gaq_ref[...], gak_ref[...], gbq_ref[...], gbk_ref[...]
    a_scale = A_QK_DIM ** -0.5
    i_scale = IDX_DIM ** -0.5

    for h in range(A_HEADS):
        q = _seg_rms(p_ref[0, :, OFF_AQ + h * LANES:OFF_AQ + (h + 1) * LANES], ones64, A_QK_DIM)
        q = q * gaq * a_scale
        qa2_ref[0, h, 0] = jnp.where(lo_half, q, 0.0).astype(qa2_ref.dtype)
        qa2_ref[0, h, 1] = jnp.where(lo_half, 0.0, q).astype(qa2_ref.dtype)
        k = _seg_rms(p_ref[0, :, OFF_AK + h * LANES:OFF_AK + (h + 1) * LANES], ones64, A_QK_DIM)
        ka_ref[0, h] = (k * gak).astype(ka_ref.dtype)
        va_ref[0, h] = p_ref[0, :, OFF_AV + h * LANES:OFF_AV + (h + 1) * LANES].astype(va_ref.dtype)
    for h in range(B_HEADS):
        q = _seg_rms(p_ref[0, :, OFF_BQ + h * LANES:OFF_BQ + (h + 1) * LANES], ones128, B_HEAD_DIM)
        qb_ref[0, h] = (q * gbq).astype(qb_ref.dtype)
    k = _seg_rms(p_ref[0, :, OFF_BK:OFF_BK + B_K], ones128, B_HEAD_DIM)
    kb_ref[0] = (k * gbk).astype(kb_ref.dtype)
    vb_ref[0] = p_ref[0, :, OFF_BV:OFF_BV + B_V].astype(vb_ref.dtype)
    for hp in range(IDX_HEADS // 2):
        qi = p_ref[0, :, OFF_IQ + hp * LANES:OFF_IQ + (hp + 1) * LANES] * i_scale
        qi2_ref[0, 2 * hp] = jnp.where(lo_half, qi, 0.0).astype(qi2_ref.dtype)
        qi2_ref[0, 2 * hp + 1] = jnp.where(lo_half, 0.0, qi).astype(qi2_ref.dtype)
    kt = p_ref[0, :, OFF_IK:OFF_IK + LANES]
    kt = jnp.where(lo_half, kt, 0.0)
    ki2_ref[0] = (kt + pltpu.roll(kt, IDX_DIM, 1)).astype(ki2_ref.dtype)


def _prep(proj3, gaq, gak, gbq, gbk):
    b, s, npad = proj3.shape
    ts = _pick(s, 256)
    dt = MXU_DTYPE
    head4 = lambda nh: jax.ShapeDtypeStruct((b, nh, s, LANES), dt)
    flat3 = jax.ShapeDtypeStruct((b, s, LANES), dt)
    hspec = lambda nh: pl.BlockSpec((1, nh, ts, LANES), lambda bi, i: (bi, 0, i, 0))
    fspec = pl.BlockSpec((1, ts, LANES), lambda bi, i: (bi, i, 0))
    gspec = pl.BlockSpec((1, LANES), lambda bi, i: (0, 0))
    return pl.pallas_call(
        _prep_kernel,
        out_shape=(jax.ShapeDtypeStruct((b, A_HEADS, 2, s, LANES), dt),
                   head4(A_HEADS), head4(A_HEADS), head4(B_HEADS), flat3, flat3,
                   head4(IDX_HEADS), flat3),
        grid=(b, s // ts),
        in_specs=[pl.BlockSpec((1, ts, npad), lambda bi, i: (bi, i, 0)), gspec, gspec, gspec, gspec],
        out_specs=(pl.BlockSpec((1, A_HEADS, 2, ts, LANES), lambda bi, i: (bi, 0, 0, i, 0)),
                   hspec(A_HEADS), hspec(A_HEADS), hspec(B_HEADS), fspec, fspec,
                   hspec(IDX_HEADS), fspec),
        compiler_params=_cparams("parallel", "parallel"),
        name="head_prep",
    )(proj3, gaq, gak, gbq, gbk)


def _rel_bucket_np(rel):
    nb = N_BUCKETS // 2
    max_exact = nb // 2
    bucket = np.where(rel > 0, nb, 0)
    n = np.abs(rel)
    nf = np.maximum(n, 1).astype(np.float32)
    large = max_exact + (np.log(nf / np.float32(max_exact)) / np.float32(math.log(MAX_DISTANCE / max_exact))
                         * np.float32(nb - max_exact)).astype(np.int32)
    large = np.minimum(large, nb - 1)
    return (bucket + np.where(n < max_exact, n, large)).astype(np.int32)


def _near_bucket_ids(t):
    tq = np.arange(t)[:, None]
    out = []
    for key0 in (0, -t):
        ts = key0 + np.arange(2 * t)[None, :]
        ids = _rel_bucket_np(ts - tq)
        vis = np.floor_divide(ts, CHUNK) <= np.floor_divide(tq, CHUNK)
        out.append(np.where(vis, ids, -1))
    return np.stack(out).astype(np.int32)


FAR_BUCKET = N_BUCKETS // 2 - 1


def _bias_kernel(tab_ref, ids_ref, o_ref):
    h = pl.program_id(0)
    ids = ids_ref[...]
    far = tab_ref[FAR_BUCKET, h]
    acc = jnp.full(ids.shape, NEG_INF, jnp.float32)
    for bkt in range(N_BUCKETS):
        acc = jnp.where(ids == bkt, tab_ref[bkt, h] - far, acc)
    o_ref[0] = acc


def _near_bias(rel_bias, t):
    ids = jnp.asarray(_near_bucket_ids(t))
    nh = rel_bias.shape[1]
    return pl.pallas_call(
        _bias_kernel,
        out_shape=jax.ShapeDtypeStruct((nh, 2, t, 2 * t), jnp.float32),
        grid=(nh,),
        in_specs=[pl.BlockSpec(memory_space=pltpu.SMEM),
                  pl.BlockSpec((2, t, 2 * t), lambda h: (0, 0, 0))],
        out_specs=pl.BlockSpec((1, 2, t, 2 * t), lambda h: (h, 0, 0, 0)),
        compiler_params=_cparams("arbitrary"),
        name="near_bias",
    )(rel_bias, ids)


def _softmax_step(s, v, m_sc, l_sc, acc_sc):
    m_prev = m_sc[...]
    m_new = jnp.maximum(m_prev, jnp.max(s, axis=-1, keepdims=True))
    alpha = jnp.exp(m_prev - m_new)
    p = jnp.exp(s - m_new)
    l_sc[...] = alpha * l_sc[...] + jnp.sum(p, axis=-1, keepdims=True)
    acc_sc[...] = alpha * acc_sc[...] + _dot(p.astype(v.dtype), v)
    m_sc[...] = m_new


def _diff_kernel(q2_ref, k_ref, v_ref, bias_ref, lam_ref, g_ref, o_ref, m_sc, l_sc, acc_sc,
                 *, lam_init):
    t = ATT_T
    i = pl.program_id(2)
    q2 = q2_ref[0, 0].reshape(2 * t, LANES)
    near0 = pl.multiple_of(jnp.maximum(i - 1, 0) * t, t)

    m_sc[...] = jnp.full(m_sc.shape, M_FLOOR, jnp.float32)
    l_sc[...] = jnp.zeros(l_sc.shape, jnp.float32)
    acc_sc[...] = jnp.zeros(acc_sc.shape, jnp.float32)

    kn = k_ref[0, 0, pl.ds(near0, 2 * t), :]
    vn = v_ref[0, 0, pl.ds(near0, 2 * t), :]
    s = _dot_nt(q2, kn).reshape(2, t, 2 * t) + bias_ref[0, 0][None]
    _softmax_step(s.reshape(2 * t, 2 * t), vn, m_sc, l_sc, acc_sc)

    def far_body(j, carry):
        k0 = pl.multiple_of(j * t, t)
        kf = k_ref[0, 0, pl.ds(k0, t), :]
        vf = v_ref[0, 0, pl.ds(k0, t), :]
        _softmax_step(_dot_nt(q2, kf), vf, m_sc, l_sc, acc_sc)
        return carry

    lax.fori_loop(0, jnp.maximum(i - 1, 0), far_body, 0)

    lq = lam_ref[...]
    lam = (jnp.exp(jnp.sum(lq[0:1] * lq[1:2], axis=-1, keepdims=True))
           - jnp.exp(jnp.sum(lq[2:3] * lq[3:4], axis=-1, keepdims=True)) + lam_init)
    o = acc_sc[...] / l_sc[...]
    o = o[:t] - lam * o[t:]
    ms = jnp.mean(o * o, axis=-1, keepdims=True)
    o = o * lax.rsqrt(ms + EPS) * g_ref[...]
    o_ref[...] = (o * (1.0 - lam_init)).astype(o_ref.dtype)


def _diff_attention(qa2, ka, va, bias, lam_qk, g_out, lam_init):
    b, nh, _, s, _ = qa2.shape
    t = ATT_T
    nt = s // t
    return pl.pallas_call(
        functools.partial(_diff_kernel, lam_init=lam_init),
        out_shape=jax.ShapeDtypeStruct((b * s, nh * A_V_DIM), MXU_DTYPE),
        grid=(b, nh, nt),
        in_specs=[pl.BlockSpec((1, 1, 2, t, LANES), lambda bi, h, i: (bi, h, 0, i, 0)),
                  pl.BlockSpec((1, 1, s, LANES), lambda bi, h, i: (bi, h, 0, 0)),
                  pl.BlockSpec((1, 1, s, LANES), lambda bi, h, i: (bi, h, 0, 0)),
                  pl.BlockSpec((1, 1, t, 2 * t), lambda bi, h, i: (h, jnp.minimum(i, 1), 0, 0)),
                  pl.BlockSpec((4, A_QK_DIM), lambda bi, h, i: (0, 0)),
                  pl.BlockSpec((1, A_V_DIM), lambda bi, h, i: (0, 0))],
        out_specs=pl.BlockSpec((t, A_V_DIM), lambda bi, h, i: (bi * nt + i, h)),
        scratch_shapes=[pltpu.VMEM((2 * t, 1), jnp.float32),
                        pltpu.VMEM((2 * t, 1), jnp.float32),
                        pltpu.VMEM((2 * t, A_V_DIM), jnp.float32)],
        compiler_params=_cparams("parallel", "parallel", "arbitrary"),
        name="diff_attention",
    )(qa2, ka, va, bias, lam_qk, g_out.reshape(1, A_V_DIM))


def _sortable_key(x):
    bits = pltpu.bitcast(x, jnp.int32)
    return bits ^ ((bits >> 31) & 0x7FFFFFFF)


def _dsa_kernel(qi2_ref, ki2_ref, wi_ref, qb_ref, kb_ref, vb_ref, bias_ref, o_ref,
                key_sc, am_sc, m_sc, l_sc, acc_sc, *, topk, heads_per_group):
    t = ATT_T
    kb_sz = IDX_KB
    i = pl.program_id(1)
    q0 = i * t
    n_idx_blocks = (q0 + t + kb_sz - 1) // kb_sz
    t_idx = q0 + lax.broadcasted_iota(jnp.int32, (1, t), 1)
    t_chunk = t_idx >> CHUNK_SHIFT

    wi = wi_ref[0] * (IDX_HEADS ** -0.5)

    def score_body(kbi, carry):
        k0 = pl.multiple_of(kbi * kb_sz, kb_sz)
        kblk = ki2_ref[0, pl.ds(k0, kb_sz), :]
        sc = jnp.zeros((kb_sz, t), jnp.float32)
        for h in range(IDX_HEADS):
            d = _dot_nt(kblk, qi2_ref[0, h])
            sc = sc + jnp.maximum(d, 0.0) * wi[h:h + 1, :]
        s_chunk = (k0 + lax.broadcasted_iota(jnp.int32, (kb_sz, 1), 0)) >> CHUNK_SHIFT
        sc = jnp.where(s_chunk <= t_chunk, sc, NEG_INF)
        key_sc[pl.ds(k0, kb_sz), :] = _sortable_key(sc)
        return carry

    lax.fori_loop(0, n_idx_blocks, score_body, 0)

    kk = jnp.minimum((t_chunk + 1) * CHUNK, topk)

    def count_ge(cand):
        def body(kbi, c):
            k0 = pl.multiple_of(kbi * kb_sz, kb_sz)
            ge = (key_sc[pl.ds(k0, kb_sz), :] >= cand).astype(jnp.int32)
            return c + jnp.sum(ge.reshape(kb_sz // 8, 8, t), axis=0)
        c8 = lax.fori_loop(0, n_idx_blocks, body, jnp.zeros((8, t), jnp.int32))
        return jnp.sum(c8, axis=0, keepdims=True)

    zero = jnp.zeros((1, t), jnp.int32)
    thr0 = jnp.where(count_ge(zero) >= kk, zero, jnp.full((1, t), INT_MIN, jnp.int32))

    def bit_body(it, thr):
        cand = thr + jnp.left_shift(jnp.int32(1), 30 - it)
        return jnp.where(count_ge(cand) >= kk, cand, thr)

    thr = lax.fori_loop(0, 31, bit_body, thr0)

    def mask_body(kbi, carry):
        k0 = pl.multiple_of(kbi * kb_sz, kb_sz)
        sel = jnp.where(key_sc[pl.ds(k0, kb_sz), :] >= thr, 0.0, NEG_INF)
        am_sc[:, pl.ds(k0, kb_sz)] = sel.T
        return carry

    lax.fori_loop(0, n_idx_blocks, mask_body, 0)

    g = heads_per_group
    near0 = pl.multiple_of(jnp.maximum(i - 1, 0) * t, t)
    scale = B_HEAD_DIM ** -0.5
    for hg in range(B_HEADS // g):
        q = qb_ref[0, hg * g:(hg + 1) * g].reshape(g * t, LANES)
        m_sc[...] = jnp.full(m_sc.shape, M_FLOOR, jnp.float32)
        l_sc[...] = jnp.zeros(l_sc.shape, jnp.float32)
        acc_sc[...] = jnp.zeros(acc_sc.shape, jnp.float32)

        kn = kb_ref[0, pl.ds(near0, 2 * t), :]
        vn = vb_ref[0, pl.ds(near0, 2 * t), :]
        s = _dot_nt(q, kn).reshape(g, t, 2 * t) * scale
        s = s + (bias_ref[hg * g:(hg + 1) * g, 0] + am_sc[:, pl.ds(near0, 2 * t)][None])
        _softmax_step(s.reshape(g * t, 2 * t), vn, m_sc, l_sc, acc_sc)

        def far_body(j, carry):
            k0 = pl.multiple_of(j * t, t)
            kf = kb_ref[0, pl.ds(k0, t), :]
            vf = vb_ref[0, pl.ds(k0, t), :]
            sf = _dot_nt(q, kf).reshape(g, t, t) * scale + am_sc[:, pl.ds(k0, t)][None]
            _softmax_step(sf.reshape(g * t, t), vf, m_sc, l_sc, acc_sc)
            return carry

        lax.fori_loop(0, jnp.maximum(i - 1, 0), far_body, 0)

        o = acc_sc[...] / l_sc[...]
        for hh in range(g):
            h = hg * g + hh
            o_ref[:, h * B_HEAD_DIM:(h + 1) * B_HEAD_DIM] = o[hh * t:(hh + 1) * t].astype(o_ref.dtype)


def _dsa_attention(qi2, ki2, wi_t, qb, kb, vb, bias, topk):
    b, nh, s, _ = qb.shape
    t = ATT_T
    nt = s // t
    g = 4
    nb_cols = B_HEADS * B_HEAD_DIM
    return pl.pallas_call(
        functools.partial(_dsa_kernel, topk=topk, heads_per_group=g),
        out_shape=jax.ShapeDtypeStruct((b * s, nb_cols), MXU_DTYPE),
        grid=(b, nt),
        in_specs=[pl.BlockSpec((1, IDX_HEADS, t, LANES), lambda bi, i: (bi, 0, i, 0)),
                  pl.BlockSpec((1, s, LANES), lambda bi, i: (bi, 0, 0)),
                  pl.BlockSpec((1, IDX_HEADS, t), lambda bi, i: (bi, 0, i)),
                  pl.BlockSpec((1, nh, t, LANES), lambda bi, i: (bi, 0, i, 0)),
                  pl.BlockSpec((1, s, LANES), lambda bi, i: (bi, 0, 0)),
                  pl.BlockSpec((1, s, LANES), lambda bi, i: (bi, 0, 0)),
                  pl.BlockSpec((B_HEADS, 1, t, 2 * t), lambda bi, i: (1, jnp.minimum(i, 1), 0, 0))],
        out_specs=pl.BlockSpec((t, nb_cols), lambda bi, i: (bi * nt + i, 0)),
        scratch_shapes=[pltpu.VMEM((s, t), jnp.int32),
                        pltpu.VMEM((t, s), jnp.float32),
                        pltpu.VMEM((g * t, 1), jnp.float32),
                        pltpu.VMEM((g * t, 1), jnp.float32),
                        pltpu.VMEM((g * t, B_HEAD_DIM), jnp.float32)],
        compiler_params=_cparams("parallel", "arbitrary"),
        name="dsa_attention",
    )(qi2, ki2, wi_t, qb, kb, vb, bias)


def _ffn1_kernel(h_ref, hp_ref, wg_ref, wu_ref, cw_ref, cb_ref, o_ref, *, tiles_per_seq):
    i = pl.program_id(0)
    tm = h_ref.shape[0]
    h = h_ref[...]
    wg = wg_ref[...]
    gate = _dot(h, wg)
    up = _dot(h, wu_ref[...])
    prev = _dot(hp_ref[...], wg)
    prev = jnp.where(i % tiles_per_seq == 0, 0.0, prev)
    row = lax.broadcasted_iota(jnp.int32, (tm, 1), 0)
    g1 = jnp.where(row == 0, prev[7:8], pltpu.roll(gate, 1, 0))
    g2 = pltpu.roll(gate, 2, 0)
    g2 = jnp.where(row == 0, prev[6:7], jnp.where(row == 1, prev[7:8], g2))
    cw = cw_ref[...]
    gc = cw[0:1] * g2 + cw[1:2] * g1 + cw[2:3] * gate + cb_ref[...]
    act = gc * (1.0 / (1.0 + jnp.exp(-gc))) * up
    o_ref[...] = act.astype(o_ref.dtype)


def _ffn1(h, w_gu, conv_w, conv_b, seq):
    m, d = h.shape
    dff = conv_w.shape[1]
    tm = _pick(seq, 512)
    tn = _pick(dff, 256)
    nj = dff // tn
    sub = 8
    return pl.pallas_call(
        functools.partial(_ffn1_kernel, tiles_per_seq=seq // tm),
        out_shape=jax.ShapeDtypeStruct((m, dff), MXU_DTYPE),
        grid=(m // tm, nj),
        in_specs=[pl.BlockSpec((tm, d), lambda i, j: (i, 0)),
                  pl.BlockSpec((sub, d), lambda i, j: (jnp.maximum(i * (tm // sub) - 1, 0), 0)),
                  pl.BlockSpec((d, tn), lambda i, j: (0, j)),
                  pl.BlockSpec((d, tn), lambda i, j: (0, j + nj)),
                  pl.BlockSpec((CONV_WIDTH, tn), lambda i, j: (0, j)),
                  pl.BlockSpec((1, tn), lambda i, j: (0, j))],
        out_specs=pl.BlockSpec((tm, tn), lambda i, j: (i, j)),
        compiler_params=_cparams("parallel", "arbitrary"),
        name="ffn_gate_up_glu",
    )(h, h, w_gu, w_gu, conv_w, conv_b.reshape(1, dff))


def kernel(x, attn_norm, w_in, a_q_norm, a_k_norm, lambda_qk, a_out_norm, b_q_norm, b_k_norm,
           rel_bias, w_out, ffn_norm, w_gate_up, conv_w, conv_b, w_down):
    bsz, s_len, d_model = x.shape
    depth = w_in.shape[0]
    m = bsz * s_len
    topk = min(TOPK_MAX, s_len // 4)
    assert s_len % (2 * ATT_T) == 0 and s_len % IDX_KB == 0
    assert w_in.shape[2] == D_IN and w_out.shape[1] == A_V + B_Q

    near_bias = _near_bias(rel_bias, ATT_T)
    x2 = x.reshape(m, d_model)
    for l in range(depth):
        lam_init = 0.8 - 0.6 * math.exp(-0.3 * l)
        w_in_l = jnp.pad(w_in[l].astype(MXU_DTYPE), ((0, 0), (0, D_IN_PAD - D_IN)))
        h = _rmsnorm(x2, attn_norm[l])
        proj = _matmul(h, w_in_l, tm_pref=1024, tn_pref=512, name="in_proj")
        proj3 = proj.reshape(bsz, s_len, D_IN_PAD)
        two = lambda g: jnp.concatenate([g, g]).reshape(1, LANES)
        qa2, ka, va, qb, kb, vb, qi2, ki2 = _prep(
            proj3, two(a_q_norm[l]), two(a_k_norm[l]),
            b_q_norm[l].reshape(1, LANES), b_k_norm[l].reshape(1, LANES))
        wi_t = jnp.swapaxes(proj3[:, :, OFF_IW:OFF_IW + I_W], 1, 2)
        mix_a = _diff_attention(qa2, ka, va, near_bias, lambda_qk[l], a_out_norm[l], lam_init)
        mix_b = _dsa_attention(qi2, ki2, wi_t, qb, kb, vb, near_bias, topk)
        x2 = _matmul2_res(mix_a, mix_b, w_out[l].astype(MXU_DTYPE), x2,
                          tm_pref=1024, tn_pref=512, name="out_proj")
        h = _rmsnorm(x2, ffn_norm[l])
        act = _ffn1(h, w_gate_up[l].astype(MXU_DTYPE), conv_w[l], conv_b[l], s_len)
        x2 = _matmul(act, w_down[l].astype(MXU_DTYPE), x2, tm_pref=512, tn_pref=256, name="down_proj")
    return x2.reshape(bsz, s_len, d_model)
```

```python
import functools
import math

import numpy as np
import jax
import jax.numpy as jnp
from jax import lax
from jax.experimental import pallas as pl
from jax.experimental.pallas import tpu as pltpu

CHUNK = 64
CHUNK_SHIFT = 6
A_HEADS = 16
A_QK_DIM = 64
A_V_DIM = 128
B_HEADS = 16
B_HEAD_DIM = 128
IDX_HEADS = 16
IDX_DIM = 64
TOPK_MAX = 256
N_BUCKETS = 32
MAX_DISTANCE = 128
CONV_WIDTH = 3
EPS = 1e-6

A_Q = A_HEADS * 2 * A_QK_DIM
A_K = A_HEADS * 2 * A_QK_DIM
A_V = A_HEADS * A_V_DIM
B_Q = B_HEADS * B_HEAD_DIM
B_K = B_HEAD_DIM
B_V = B_HEAD_DIM
I_Q = IDX_HEADS * IDX_DIM
I_K = IDX_DIM
I_W = IDX_HEADS
OFF_AQ = 0
OFF_AK = OFF_AQ + A_Q
OFF_AV = OFF_AK + A_K
OFF_BQ = OFF_AV + A_V
OFF_BK = OFF_BQ + B_Q
OFF_BV = OFF_BK + B_K
OFF_IQ = OFF_BV + B_V
OFF_IK = OFF_IQ + I_Q
OFF_IW = OFF_IK + I_K
D_IN = OFF_IW + I_W

LANES = 128
V7X_VMEM_LIMIT = 56 * 1024 * 1024

DIFF_T = 256
DIFF_HEADS_PER_STEP = 2
ATT_T = 128
IDX_KB = 256
DSA_FAR_W = 512
DSA_HEADS_PER_GROUP = 4
MXU_DTYPE = jnp.bfloat16

NEG_INF = float("-inf")
M_FLOOR = -1e30
INT_MIN = -(2 ** 31)


def _cparams(*sem):
    return pltpu.CompilerParams(dimension_semantics=sem, vmem_limit_bytes=V7X_VMEM_LIMIT)


def _pick(n, pref):
    t = min(pref, n)
    while n % t:
        t -= LANES
    return t


def _dot(a, b):
    return jnp.dot(a, b, preferred_element_type=jnp.float32)


def _dot_nt(a, b):
    return lax.dot_general(a, b, (((1,), (1,)), ((), ())), preferred_element_type=jnp.float32)


def _rmsnorm_kernel(x_ref, g_ref, o_ref):
    x = x_ref[...]
    ms = jnp.mean(x * x, axis=-1, keepdims=True)
    o_ref[...] = (x * lax.rsqrt(ms + EPS) * g_ref[...]).astype(o_ref.dtype)


def _rmsnorm(x2d, g):
    m, d = x2d.shape
    tm = _pick(m, 256)
    return pl.pallas_call(
        _rmsnorm_kernel,
        out_shape=jax.ShapeDtypeStruct((m, d), MXU_DTYPE),
        grid=(m // tm,),
        in_specs=[pl.BlockSpec((tm, d), lambda i: (i, 0)),
                  pl.BlockSpec((1, d), lambda i: (0, 0))],
        out_specs=pl.BlockSpec((tm, d), lambda i: (i, 0)),
        compiler_params=_cparams("parallel"),
        name="rmsnorm",
    )(x2d, g.reshape(1, d))


def _mm_kernel(a_ref, w_ref, o_ref):
    o_ref[...] = _dot(a_ref[...], w_ref[...].astype(MXU_DTYPE)).astype(o_ref.dtype)


def _mm_res_kernel(a_ref, w_ref, r_ref, o_ref):
    o_ref[...] = (r_ref[...] + _dot(a_ref[...], w_ref[...].astype(MXU_DTYPE))).astype(o_ref.dtype)


def _mm2_res_kernel(a1_ref, a2_ref, w1_ref, w2_ref, r_ref, o_ref):
    acc = (_dot(a1_ref[...], w1_ref[...].astype(MXU_DTYPE))
           + _dot(a2_ref[...], w2_ref[...].astype(MXU_DTYPE)))
    o_ref[...] = (r_ref[...] + acc).astype(o_ref.dtype)


def _matmul2_res(a1, a2, w_all, layer, res, *, tm_pref, tn_pref, name):
    m, k = a1.shape
    n = w_all.shape[2]
    tm, tn = _pick(m, tm_pref), _pick(n, tn_pref)
    return pl.pallas_call(
        _mm2_res_kernel,
        out_shape=jax.ShapeDtypeStruct((m, n), jnp.float32),
        grid=(m // tm, n // tn),
        in_specs=[pl.BlockSpec((tm, k), lambda i, j: (i, 0)),
                  pl.BlockSpec((tm, k), lambda i, j: (i, 0)),
                  pl.BlockSpec((None, k, tn), lambda i, j: (layer, 0, j)),
                  pl.BlockSpec((None, k, tn), lambda i, j: (layer, 1, j)),
                  pl.BlockSpec((tm, tn), lambda i, j: (i, j))],
        out_specs=pl.BlockSpec((tm, tn), lambda i, j: (i, j)),
        compiler_params=_cparams("parallel", "arbitrary"),
        name=name,
    )(a1, a2, w_all, w_all, res)


def _matmul(a, w_all, layer, res=None, *, tm_pref, tn_pref, name):
    m, k = a.shape
    n = w_all.shape[2]
    tm, tn = _pick(m, tm_pref), min(tn_pref, n)
    in_specs = [pl.BlockSpec((tm, k), lambda i, j: (i, 0)),
                pl.BlockSpec((None, k, tn), lambda i, j: (layer, 0, j))]
    args = [a, w_all]
    kern = _mm_kernel
    if res is not None:
        in_specs.append(pl.BlockSpec((tm, tn), lambda i, j: (i, j)))
        args.append(res)
        kern = _mm_res_kernel
    return pl.pallas_call(
        kern,
        out_shape=jax.ShapeDtypeStruct((m, n), jnp.float32),
        grid=(m // tm, pl.cdiv(n, tn)),
        in_specs=in_specs,
        out_specs=pl.BlockSpec((tm, tn), lambda i, j: (i, j)),
        compiler_params=_cparams("parallel", "arbitrary"),
        name=name,
    )(*args)


def _seg_ones(seg):
    r = lax.broadcasted_iota(jnp.int32, (LANES, LANES), 0) // seg
    c = lax.broadcasted_iota(jnp.int32, (LANES, LANES), 1) // seg
    return (r == c).astype(MXU_DTYPE)


def _seg_rms(x, ones, seg):
    sq = x * x
    hi = sq.astype(MXU_DTYPE)
    lo = (sq - hi.astype(jnp.float32)).astype(MXU_DTYPE)
    ss = _dot(hi, ones) + _dot(lo, ones)
    return x * lax.rsqrt(ss * (1.0 / seg) + EPS)


def _prep_kernel(p_ref, gaq_ref, gak_ref, gbq_ref, gbk_ref,
                 qa2_ref, ka_ref, va_ref, qb_ref, kb_ref, vb_ref, qi2_ref, ki2_ref):
    ones64 = _seg_ones(A_QK_DIM)
    ones128 = _seg_ones(B_HEAD_DIM)
    lane = lax.broadcasted_iota(jnp.int32, (1, LANES), 1)
    lo_half = lane < A_QK_DIM
    gaq, gak, gbq, gbk = gaq_ref[...], gak_ref[...], gbq_ref[...], gbk_ref[...]
    a_scale = A_QK_DIM ** -0.5
    i_scale = IDX_DIM ** -0.5

    for h in range(A_HEADS):
        q = _seg_rms(p_ref[0, :, OFF_AQ + h * LANES:OFF_AQ + (h + 1) * LANES], ones64, A_QK_DIM)
        q = q * gaq * a_scale
        qa2_ref[0, h, 0] = jnp.where(lo_half, q, 0.0).astype(qa2_ref.dtype)
        qa2_ref[0, h, 1] = jnp.where(lo_half, 0.0, q).astype(qa2_ref.dtype)
        k = _seg_rms(p_ref[0, :, OFF_AK + h * LANES:OFF_AK + (h + 1) * LANES], ones64, A_QK_DIM)
        ka_ref[0, h] = (k * gak).astype(ka_ref.dtype)
        va_ref[0, h] = p_ref[0, :, OFF_AV + h * LANES:OFF_AV + (h + 1) * LANES].astype(va_ref.dtype)
    for h in range(B_HEADS):
        q = _seg_rms(p_ref[0, :, OFF_BQ + h * LANES:OFF_BQ + (h + 1) * LANES], ones128, B_HEAD_DIM)
        qb_ref[0, h] = (q * gbq).astype(qb_ref.dtype)
    k = _seg_rms(p_ref[0, :, OFF_BK:OFF_BK + B_K], ones128, B_HEAD_DIM)
    kb_ref[0] = (k * gbk).astype(kb_ref.dtype)
    vb_ref[0] = p_ref[0, :, OFF_BV:OFF_BV + B_V].astype(vb_ref.dtype)
    for hp in range(IDX_HEADS // 2):
        qi = p_ref[0, :, OFF_IQ + hp * LANES:OFF_IQ + (hp + 1) * LANES] * i_scale
        qi2_ref[0, 2 * hp] = jnp.where(lo_half, qi, 0.0).astype(qi2_ref.dtype)
        qi2_ref[0, 2 * hp + 1] = jnp.where(lo_half, 0.0, qi).astype(qi2_ref.dtype)
    kt = p_ref[0, :, OFF_IK:OFF_IK + I_K]
    ki2_ref[0] = jnp.concatenate([kt, kt], axis=-1).astype(ki2_ref.dtype)


def _prep(proj3, gaq, gak, gbq, gbk):
    b, s, npad = proj3.shape
    ts = _pick(s, 256)
    dt = MXU_DTYPE
    head4 = lambda nh: jax.ShapeDtypeStruct((b, nh, s, LANES), dt)
    flat3 = jax.ShapeDtypeStruct((b, s, LANES), dt)
    hspec = lambda nh: pl.BlockSpec((1, nh, ts, LANES), lambda bi, i: (bi, 0, i, 0))
    fspec = pl.BlockSpec((1, ts, LANES), lambda bi, i: (bi, i, 0))
    gspec = pl.BlockSpec((1, LANES), lambda bi, i: (0, 0))
    return pl.pallas_call(
        _prep_kernel,
        out_shape=(jax.ShapeDtypeStruct((b, A_HEADS, 2, s, LANES), dt),
                   head4(A_HEADS), head4(A_HEADS), head4(B_HEADS), flat3, flat3,
                   head4(IDX_HEADS), flat3),
        grid=(b, s // ts),
        in_specs=[pl.BlockSpec((1, ts, npad), lambda bi, i: (bi, i, 0)), gspec, gspec, gspec, gspec],
        out_specs=(pl.BlockSpec((1, A_HEADS, 2, ts, LANES), lambda bi, i: (bi, 0, 0, i, 0)),
                   hspec(A_HEADS), hspec(A_HEADS), hspec(B_HEADS), fspec, fspec,
                   hspec(IDX_HEADS), fspec),
        compiler_params=_cparams("parallel", "parallel"),
        name="head_prep",
    )(proj3, gaq, gak, gbq, gbk)


def _rel_bucket_np(rel):
    nb = N_BUCKETS // 2
    max_exact = nb // 2
    bucket = np.where(rel > 0, nb, 0)
    n = np.abs(rel)
    nf = np.maximum(n, 1).astype(np.float32)
    large = max_exact + (np.log(nf / np.float32(max_exact)) / np.float32(math.log(MAX_DISTANCE / max_exact))
                         * np.float32(nb - max_exact)).astype(np.int32)
    large = np.minimum(large, nb - 1)
    return (bucket + np.where(n < max_exact, n, large)).astype(np.int32)


def _near_bucket_ids(t):
    tq = np.arange(t)[:, None]
    out = []
    for key0 in (0, -t):
        ts = key0 + np.arange(2 * t)[None, :]
        ids = _rel_bucket_np(ts - tq)
        vis = np.floor_divide(ts, CHUNK) <= np.floor_divide(tq, CHUNK)
        out.append(np.where(vis, ids, -1))
    return np.stack(out).astype(np.int32)


FAR_BUCKET = N_BUCKETS // 2 - 1


def _bias_kernel(tab_ref, ids_ref, o_ref):
    h = pl.program_id(0)
    ids = ids_ref[...]
    far = tab_ref[FAR_BUCKET, h]
    acc = jnp.full(ids.shape, NEG_INF, jnp.float32)
    for bkt in range(N_BUCKETS):
        acc = jnp.where(ids == bkt, tab_ref[bkt, h] - far, acc)
    o_ref[0] = acc


def _near_bias(rel_bias, t):
    ids = jnp.asarray(_near_bucket_ids(t))
    nh = rel_bias.shape[1]
    return pl.pallas_call(
        _bias_kernel,
        out_shape=jax.ShapeDtypeStruct((nh, 2, t, 2 * t), jnp.float32),
        grid=(nh,),
        in_specs=[pl.BlockSpec(memory_space=pltpu.SMEM),
                  pl.BlockSpec((2, t, 2 * t), lambda h: (0, 0, 0))],
        out_specs=pl.BlockSpec((1, 2, t, 2 * t), lambda h: (h, 0, 0, 0)),
        compiler_params=_cparams("arbitrary"),
        name="near_bias",
    )(rel_bias, ids)


def _softmax_step(s, v, m_sc, l_sc, acc_sc, idx, first):
    rows, width = s.shape
    blocks = [s[:, c * LANES:(c + 1) * LANES] for c in range(width // LANES)]
    m_part = functools.reduce(jnp.maximum, blocks)
    m_cur = jnp.broadcast_to(jnp.max(m_part, axis=-1, keepdims=True), (rows, LANES))
    if first:
        m_new = jnp.maximum(m_cur, M_FLOOR)
    else:
        m_prev = m_sc[idx]
        m_new = jnp.maximum(m_prev, m_cur)
        alpha = jnp.exp(m_prev - m_new)
    ps = [jnp.exp(blk - m_new) for blk in blocks]
    l_part = functools.reduce(jnp.add, ps)
    l_cur = jnp.broadcast_to(jnp.sum(l_part, axis=-1, keepdims=True), (rows, LANES))
    p = jnp.concatenate([x.astype(v.dtype) for x in ps], axis=-1)
    pv = _dot(p, v)
    if first:
        l_sc[idx] = l_cur
        acc_sc[idx] = pv
    else:
        l_sc[idx] = alpha * l_sc[idx] + l_cur
        acc_sc[idx] = alpha * acc_sc[idx] + pv
    m_sc[idx] = m_new


def _diff_kernel(q2_ref, k_ref, v_ref, bias_ref, lam_ref, g_ref, o_ref, m_sc, l_sc, acc_sc,
                 *, lam_init, t, hg):
    i = pl.program_id(2)
    n_far = jnp.maximum(i - 1, 0)
    near0 = pl.multiple_of(n_far * t, t)

    def step(k0, width, first):
        for hh in range(hg):
            q2 = q2_ref[0, hh].reshape(2 * t, LANES)
            s = _dot_nt(q2, k_ref[0, hh, pl.ds(k0, width), :])
            if first:
                s = (s.reshape(2, t, width) + bias_ref[hh, 0][None]).reshape(2 * t, width)
            _softmax_step(s, v_ref[0, hh, pl.ds(k0, width), :], m_sc, l_sc, acc_sc, hh, first)

    step(near0, 2 * t, True)

    def far_body(j, carry):
        step(pl.multiple_of(j * 2 * t, 2 * t), 2 * t, False)
        return carry

    lax.fori_loop(0, n_far // 2, far_body, 0)

    @pl.when(n_far % 2 == 1)
    def _():
        step(pl.multiple_of((n_far - 1) * t, t), t, False)

    lq = lam_ref[...]
    lam = (jnp.exp(jnp.sum(lq[0:1] * lq[1:2], axis=-1, keepdims=True))
           - jnp.exp(jnp.sum(lq[2:3] * lq[3:4], axis=-1, keepdims=True)) + lam_init)
    for hh in range(hg):
        o = acc_sc[hh] / l_sc[hh]
        o = o[:t] - lam * o[t:]
        ms = jnp.mean(o * o, axis=-1, keepdims=True)
        o = o * lax.rsqrt(ms + EPS) * g_ref[...]
        o_ref[:, hh * A_V_DIM:(hh + 1) * A_V_DIM] = (o * (1.0 - lam_init)).astype(o_ref.dtype)


def _diff_attention(qa2, ka, va, bias, lam_qk, g_out, lam_init):
    b, nh, _, s, _ = qa2.shape
    t = DIFF_T
    hg = DIFF_HEADS_PER_STEP
    nt = s // t
    rows = 2 * t
    return pl.pallas_call(
        functools.partial(_diff_kernel, lam_init=lam_init, t=t, hg=hg),
        out_shape=jax.ShapeDtypeStruct((b * s, nh * A_V_DIM), MXU_DTYPE),
        grid=(b, nh // hg, nt),
        in_specs=[pl.BlockSpec((1, hg, 2, t, LANES), lambda bi, h, i: (bi, h, 0, i, 0)),
                  pl.BlockSpec((1, hg, s, LANES), lambda bi, h, i: (bi, h, 0, 0)),
                  pl.BlockSpec((1, hg, s, LANES), lambda bi, h, i: (bi, h, 0, 0)),
                  pl.BlockSpec((hg, 1, t, 2 * t), lambda bi, h, i: (h, jnp.minimum(i, 1), 0, 0)),
                  pl.BlockSpec((4, A_QK_DIM), lambda bi, h, i: (0, 0)),
                  pl.BlockSpec((1, A_V_DIM), lambda bi, h, i: (0, 0))],
        out_specs=pl.BlockSpec((t, hg * A_V_DIM), lambda bi, h, i: (bi * nt + i, h)),
        scratch_shapes=[pltpu.VMEM((hg, rows, LANES), jnp.float32),
                        pltpu.VMEM((hg, rows, LANES), jnp.float32),
                        pltpu.VMEM((hg, rows, A_V_DIM), jnp.float32)],
        compiler_params=_cparams("parallel", "parallel", "arbitrary"),
        name="diff_attention",
    )(qa2, ka, va, bias, lam_qk, g_out.reshape(1, A_V_DIM))


def _sortable_key(x):
    bits = pltpu.bitcast(x, jnp.int32)
    return bits ^ ((bits >> 31) & 0x7FFFFFFF)


def _dsa_kernel(qi2_ref, ki2_ref, wi_ref, qb_ref, kb_ref, vb_ref, bias_ref, o_ref,
                key_sc, am_sc, m_sc, l_sc, acc_sc, *, topk, heads_per_group):
    t = ATT_T
    kb_sz = IDX_KB
    i = pl.program_id(1)
    q0 = i * t
    n_idx_blocks = (q0 + t + kb_sz - 1) // kb_sz
    t_idx = q0 + lax.broadcasted_iota(jnp.int32, (1, t), 1)
    t_chunk = t_idx >> CHUNK_SHIFT

    wi = wi_ref[0] * (IDX_HEADS ** -0.5)

    def score_body(kbi, carry):
        k0 = pl.multiple_of(kbi * kb_sz, kb_sz)
        kblk = ki2_ref[0, pl.ds(k0, kb_sz), :]
        sc = jnp.zeros((kb_sz, t), jnp.float32)
        for h in range(IDX_HEADS):
            d = _dot_nt(kblk, qi2_ref[0, h])
            sc = sc + jnp.maximum(d, 0.0) * wi[h:h + 1, :]
        s_chunk = (k0 + lax.broadcasted_iota(jnp.int32, (kb_sz, 1), 0)) >> CHUNK_SHIFT
        sc = jnp.where(s_chunk <= t_chunk, sc, NEG_INF)
        key_sc[pl.ds(k0, kb_sz), :] = _sortable_key(sc)
        return carry

    lax.fori_loop(0, n_idx_blocks, score_body, 0)

    kk = jnp.minimum((t_chunk + 1) * CHUNK, topk)

    def count_ge(cand):
        def body(kbi, c):
            k0 = pl.multiple_of(kbi * kb_sz, kb_sz)
            ge = (key_sc[pl.ds(k0, kb_sz), :] >= cand).astype(jnp.int32)
            return c + jnp.sum(ge.reshape(kb_sz // 8, 8, t), axis=0)
        c8 = lax.fori_loop(0, n_idx_blocks, body, jnp.zeros((8, t), jnp.int32))
        return jnp.sum(c8, axis=0, keepdims=True)

    zero = jnp.zeros((1, t), jnp.int32)
    thr0 = jnp.where(count_ge(zero) >= kk, zero, jnp.full((1, t), INT_MIN, jnp.int32))

    def bit_body(it, thr):
        cand = thr + jnp.left_shift(jnp.int32(1), 30 - it)
        return jnp.where(count_ge(cand) >= kk, cand, thr)

    thr = lax.fori_loop(0, 31, bit_body, thr0)

    def mask_body(kbi, carry):
        k0 = pl.multiple_of(kbi * kb_sz, kb_sz)
        sel = jnp.where(key_sc[pl.ds(k0, kb_sz), :] >= thr, 0.0, NEG_INF)
        am_sc[:, pl.ds(k0, kb_sz)] = sel.T
        return carry

    lax.fori_loop(0, n_idx_blocks, mask_body, 0)

    g = heads_per_group
    n_groups = B_HEADS // g
    far_w = DSA_FAR_W
    near0 = pl.multiple_of(jnp.maximum(i - 1, 0) * t, t)
    scale = B_HEAD_DIM ** -0.5
    am_near = am_sc[:, pl.ds(near0, 2 * t)]
    am_sc[:, pl.ds(near0, far_w + 2 * t)] = jnp.full((t, far_w + 2 * t), NEG_INF, jnp.float32)

    def step(k0, width, first):
        kblk = kb_ref[0, pl.ds(k0, width), :]
        vblk = vb_ref[0, pl.ds(k0, width), :]
        for hg in range(n_groups):
            q = qb_ref[0, hg * g:(hg + 1) * g].reshape(g * t, LANES)
            s = _dot_nt(q, kblk).reshape(g, t, width) * scale
            if first:
                s = s + (bias_ref[hg * g:(hg + 1) * g, 0] + am_near[None])
            else:
                s = s + am_sc[:, pl.ds(k0, width)][None]
            _softmax_step(s.reshape(g * t, width), vblk, m_sc, l_sc, acc_sc, hg, first)

    step(near0, 2 * t, True)

    def far_body(j, carry):
        step(pl.multiple_of(j * far_w, far_w), far_w, False)
        return carry

    lax.fori_loop(0, (near0 + far_w - 1) // far_w, far_body, 0)

    for hg in range(n_groups):
        o = acc_sc[hg] / l_sc[hg]
        for hh in range(g):
            h = hg * g + hh
            o_ref[:, h * B_HEAD_DIM:(h + 1) * B_HEAD_DIM] = o[hh * t:(hh + 1) * t].astype(o_ref.dtype)


def _dsa_attention(qi2, ki2, wi_t, qb, kb, vb, bias, topk):
    b, nh, s, _ = qb.shape
    t = ATT_T
    nt = s // t
    g = DSA_HEADS_PER_GROUP
    n_groups = nh // g
    nb_cols = B_HEADS * B_HEAD_DIM
    return pl.pallas_call(
        functools.partial(_dsa_kernel, topk=topk, heads_per_group=g),
        out_shape=jax.ShapeDtypeStruct((b * s, nb_cols), MXU_DTYPE),
        grid=(b, nt),
        in_specs=[pl.BlockSpec((1, IDX_HEADS, t, LANES), lambda bi, i: (bi, 0, i, 0)),
                  pl.BlockSpec((1, s, LANES), lambda bi, i: (bi, 0, 0)),
                  pl.BlockSpec((1, IDX_HEADS, t), lambda bi, i: (bi, 0, i)),
                  pl.BlockSpec((1, nh, t, LANES), lambda bi, i: (bi, 0, i, 0)),
                  pl.BlockSpec((1, s, LANES), lambda bi, i: (bi, 0, 0)),
                  pl.BlockSpec((1, s, LANES), lambda bi, i: (bi, 0, 0)),
                  pl.BlockSpec((B_HEADS, 1, t, 2 * t), lambda bi, i: (0, jnp.minimum(i, 1), 0, 0))],
        out_specs=pl.BlockSpec((t, nb_cols), lambda bi, i: (bi * nt + i, 0)),
        scratch_shapes=[pltpu.VMEM((s, t), jnp.int32),
                        pltpu.VMEM((t, s + DSA_FAR_W), jnp.float32),
                        pltpu.VMEM((n_groups, g * t, LANES), jnp.float32),
                        pltpu.VMEM((n_groups, g * t, LANES), jnp.float32),
                        pltpu.VMEM((n_groups, g * t, B_HEAD_DIM), jnp.float32)],
        compiler_params=_cparams("parallel", "arbitrary"),
        name="dsa_attention",
    )(qi2, ki2, wi_t, qb, kb, vb, bias)


def _ffn1_kernel(h_ref, hp_ref, wg_ref, wu_ref, cw_ref, cb_ref, o_ref, *, tiles_per_seq):
    i = pl.program_id(0)
    tm = h_ref.shape[0]
    h = h_ref[...]
    wg = wg_ref[...].astype(MXU_DTYPE)
    gate = _dot(h, wg)
    up = _dot(h, wu_ref[...].astype(MXU_DTYPE))
    prev = _dot(hp_ref[...], wg)
    prev = jnp.where(i % tiles_per_seq == 0, 0.0, prev)
    row = lax.broadcasted_iota(jnp.int32, (tm, 1), 0)
    g1 = jnp.where(row == 0, prev[7:8], pltpu.roll(gate, 1, 0))
    g2 = pltpu.roll(gate, 2, 0)
    g2 = jnp.where(row == 0, prev[6:7], jnp.where(row == 1, prev[7:8], g2))
    cw = cw_ref[...]
    gc = cw[0:1] * g2 + cw[1:2] * g1 + cw[2:3] * gate + cb_ref[...]
    act = gc * (1.0 / (1.0 + jnp.exp(-gc))) * up
    o_ref[...] = act.astype(o_ref.dtype)


def _ffn1(h, w_gu_all, conv_w_all, conv_b_all, layer, seq):
    m, d = h.shape
    dff = conv_w_all.shape[2]
    tm = _pick(seq, 1024)
    tn = _pick(dff, 256)
    nj = dff // tn
    sub = 8
    return pl.pallas_call(
        functools.partial(_ffn1_kernel, tiles_per_seq=seq // tm),
        out_shape=jax.ShapeDtypeStruct((m, dff), MXU_DTYPE),
        grid=(m // tm, nj),
        in_specs=[pl.BlockSpec((tm, d), lambda i, j: (i, 0)),
                  pl.BlockSpec((sub, d), lambda i, j: (jnp.maximum(i * (tm // sub) - 1, 0), 0)),
                  pl.BlockSpec((None, d, tn), lambda i, j: (layer, 0, j)),
                  pl.BlockSpec((None, d, tn), lambda i, j: (layer, 0, j + nj)),
                  pl.BlockSpec((None, CONV_WIDTH, tn), lambda i, j: (layer, 0, j)),
                  pl.BlockSpec((None, 1, tn), lambda i, j: (layer, 0, j))],
        out_specs=pl.BlockSpec((tm, tn), lambda i, j: (i, j)),
        compiler_params=_cparams("parallel", "arbitrary"),
        name="ffn_gate_up_glu",
    )(h, h, w_gu_all, w_gu_all, conv_w_all, conv_b_all.reshape(-1, 1, dff))


def kernel(x, attn_norm, w_in, a_q_norm, a_k_norm, lambda_qk, a_out_norm, b_q_norm, b_k_norm,
           rel_bias, w_out, ffn_norm, w_gate_up, conv_w, conv_b, w_down):
    bsz, s_len, d_model = x.shape
    depth = w_in.shape[0]
    m = bsz * s_len
    topk = min(TOPK_MAX, s_len // 4)
    assert s_len % (2 * DIFF_T) == 0 and s_len % DSA_FAR_W == 0 and s_len % IDX_KB == 0
    assert w_in.shape[2] == D_IN and w_out.shape[1] == A_V + B_Q

    near_bias_a = _near_bias(rel_bias[:, :A_HEADS], DIFF_T)
    near_bias_b = _near_bias(rel_bias[:, A_HEADS:], ATT_T)
    x2 = x.reshape(m, d_model)
    w_down_c = w_down.astype(MXU_DTYPE)
    for l in range(depth):
        lam_init = 0.8 - 0.6 * math.exp(-0.3 * l)
        h = _rmsnorm(x2, attn_norm[l])
        proj = _matmul(h, w_in, l, tm_pref=1024, tn_pref=512, name="in_proj")
        proj3 = proj.reshape(bsz, s_len, D_IN)
        two = lambda g: jnp.concatenate([g, g]).reshape(1, LANES)
        qa2, ka, va, qb, kb, vb, qi2, ki2 = _prep(
            proj3, two(a_q_norm[l]), two(a_k_norm[l]),
            b_q_norm[l].reshape(1, LANES), b_k_norm[l].reshape(1, LANES))
        wi_t = jnp.swapaxes(proj3[:, :, OFF_IW:OFF_IW + I_W], 1, 2)
        mix_a = _diff_attention(qa2, ka, va, near_bias_a, lambda_qk[l], a_out_norm[l], lam_init)
        mix_b = _dsa_attention(qi2, ki2, wi_t, qb, kb, vb, near_bias_b, topk)
        x2 = _matmul2_res(mix_a, mix_b, w_out, l, x2, tm_pref=1024, tn_pref=512, name="out_proj")
        h = _rmsnorm(x2, ffn_norm[l])
        act = _ffn1(h, w_gate_up, conv_w, conv_b, l, s_len)
        x2 = _matmul(act, w_down_c, l, x2, tm_pref=512, tn_pref=256, name="down_proj")
    return x2.reshape(bsz, s_len, d_model)
```

```python
import functools
import math

import numpy as np
import jax
import jax.numpy as jnp
from jax import lax
from jax.experimental import pallas as pl
from jax.experimental.pallas import tpu as pltpu

CHUNK = 64
CHUNK_SHIFT = 6
A_HEADS = 16
A_QK_DIM = 64
A_V_DIM = 128
B_HEADS = 16
B_HEAD_DIM = 128
IDX_HEADS = 16
IDX_DIM = 64
TOPK_MAX = 256
N_BUCKETS = 32
MAX_DISTANCE = 128
CONV_WIDTH = 3
EPS = 1e-6

A_Q = A_HEADS * 2 * A_QK_DIM
A_K = A_HEADS * 2 * A_QK_DIM
A_V = A_HEADS * A_V_DIM
B_Q = B_HEADS * B_HEAD_DIM
B_K = B_HEAD_DIM
B_V = B_HEAD_DIM
I_Q = IDX_HEADS * IDX_DIM
I_K = IDX_DIM
I_W = IDX_HEADS
OFF_AQ = 0
OFF_AK = OFF_AQ + A_Q
OFF_AV = OFF_AK + A_K
OFF_BQ = OFF_AV + A_V
OFF_BK = OFF_BQ + B_Q
OFF_BV = OFF_BK + B_K
OFF_IQ = OFF_BV + B_V
OFF_IK = OFF_IQ + I_Q
OFF_IW = OFF_IK + I_K
D_IN = OFF_IW + I_W

LANES = 128
V7X_VMEM_LIMIT = 56 * 1024 * 1024

DIFF_T = 256
DIFF_HEADS_PER_STEP = 2
ATT_T = 256
IDX_KB = 256
DSA_FAR_W = 512
DSA_HEADS_PER_GROUP = 2
MXU_DTYPE = jnp.bfloat16

LOG2E = math.log2(math.e)
NEG_INF = float("-inf")
M_FLOOR = -1e30
INT_MIN = -(2 ** 31)


def _cparams(*sem):
    return pltpu.CompilerParams(dimension_semantics=sem, vmem_limit_bytes=V7X_VMEM_LIMIT)


def _pick(n, pref):
    t = min(pref, n)
    while n % t:
        t -= LANES
    return t


def _dot(a, b):
    return jnp.dot(a, b, preferred_element_type=jnp.float32)


def _dot_nt(a, b):
    return lax.dot_general(a, b, (((1,), (1,)), ((), ())), preferred_element_type=jnp.float32)


def _row_sumsq(x):
    sq = functools.reduce(jnp.add, [x[:, c * LANES:(c + 1) * LANES] ** 2
                                    for c in range(x.shape[1] // LANES)])
    return jnp.broadcast_to(jnp.sum(sq, axis=-1, keepdims=True), (x.shape[0], LANES))


def _row_scale(ss_ref, d):
    return lax.rsqrt(ss_ref[:, 0:1] * (1.0 / d) + EPS)


def _emit_norm_inputs(x, j, g_ref, xg_ref, ss_ref):
    xg_ref[...] = (x * g_ref[...]).astype(xg_ref.dtype)

    @pl.when(j == 0)
    def _():
        ss_ref[...] = jnp.zeros(ss_ref.shape, jnp.float32)

    ss_ref[...] += _row_sumsq(x)


def _norm_inputs_kernel(x_ref, g_ref, xg_ref, ss_ref):
    x = x_ref[...]
    xg_ref[...] = (x * g_ref[...]).astype(xg_ref.dtype)
    ss_ref[...] = _row_sumsq(x)


def _norm_inputs(x2d, g):
    m, d = x2d.shape
    tm = _pick(m, 256)
    return pl.pallas_call(
        _norm_inputs_kernel,
        out_shape=(jax.ShapeDtypeStruct((m, d), MXU_DTYPE),
                   jax.ShapeDtypeStruct((m, LANES), jnp.float32)),
        grid=(m // tm,),
        in_specs=[pl.BlockSpec((tm, d), lambda i: (i, 0)),
                  pl.BlockSpec((1, d), lambda i: (0, 0))],
        out_specs=(pl.BlockSpec((tm, d), lambda i: (i, 0)),
                   pl.BlockSpec((tm, LANES), lambda i: (i, 0))),
        compiler_params=_cparams("parallel"),
        name="norm_inputs",
    )(x2d, g.reshape(1, d))


def _mm_nt_kernel(a_ref, ss_ref, wt_ref, o_ref):
    acc = _dot_nt(a_ref[...], wt_ref[...].astype(MXU_DTYPE))
    o_ref[...] = (acc * _row_scale(ss_ref, a_ref.shape[1])).astype(o_ref.dtype)


def _mm_res_kernel(a_ref, w_ref, r_ref, *rest, emit_norm):
    x = r_ref[...] + _dot(a_ref[...], w_ref[...])
    if emit_norm:
        g_ref, o_ref, xg_ref, ss_ref = rest
        _emit_norm_inputs(x, pl.program_id(1), g_ref, xg_ref, ss_ref)
    else:
        (o_ref,) = rest
    o_ref[...] = x


def _mm2_res_kernel(a1_ref, a2_ref, w1_ref, w2_ref, r_ref, g_ref, o_ref, xg_ref, ss_ref):
    acc = (_dot(a1_ref[...], w1_ref[...].astype(MXU_DTYPE))
           + _dot(a2_ref[...], w2_ref[...].astype(MXU_DTYPE)))
    x = r_ref[...] + acc
    _emit_norm_inputs(x, pl.program_id(1), g_ref, xg_ref, ss_ref)
    o_ref[...] = x


def _norm_out(m, n, tm, tn):
    shapes = (jax.ShapeDtypeStruct((m, n), jnp.float32),
              jax.ShapeDtypeStruct((m, n), MXU_DTYPE),
              jax.ShapeDtypeStruct((m, LANES), jnp.float32))
    specs = (pl.BlockSpec((tm, tn), lambda i, j: (i, j)),
             pl.BlockSpec((tm, tn), lambda i, j: (i, j)),
             pl.BlockSpec((tm, LANES), lambda i, j: (i, 0)))
    return shapes, specs


def _matmul2_res(a1, a2, w_all, layer, res, g_next, *, tm_pref, tn_pref, name):
    m, k = a1.shape
    n = w_all.shape[2]
    tm, tn = _pick(m, tm_pref), _pick(n, tn_pref)
    shapes, specs = _norm_out(m, n, tm, tn)
    return pl.pallas_call(
        _mm2_res_kernel,
        out_shape=shapes,
        grid=(m // tm, n // tn),
        in_specs=[pl.BlockSpec((tm, k), lambda i, j: (i, 0)),
                  pl.BlockSpec((tm, k), lambda i, j: (i, 0)),
                  pl.BlockSpec((None, k, tn), lambda i, j: (layer, 0, j)),
                  pl.BlockSpec((None, k, tn), lambda i, j: (layer, 1, j)),
                  pl.BlockSpec((tm, tn), lambda i, j: (i, j)),
                  pl.BlockSpec((1, tn), lambda i, j: (0, j))],
        out_specs=specs,
        compiler_params=_cparams("parallel", "arbitrary"),
        name=name,
    )(a1, a2, w_all, w_all, res, g_next.reshape(1, n))


def _matmul_nt(a, ss, wt_all, layer, *, tm_pref, tn_pref, name):
    m, k = a.shape
    n = wt_all.shape[1]
    tm, tn = _pick(m, tm_pref), min(tn_pref, n)
    return pl.pallas_call(
        _mm_nt_kernel,
        out_shape=jax.ShapeDtypeStruct((m, n), jnp.float32),
        grid=(m // tm, pl.cdiv(n, tn)),
        in_specs=[pl.BlockSpec((tm, k), lambda i, j: (i, 0)),
                  pl.BlockSpec((tm, LANES), lambda i, j: (i, 0)),
                  pl.BlockSpec((None, tn, k), lambda i, j: (layer, j, 0))],
        out_specs=pl.BlockSpec((tm, tn), lambda i, j: (i, j)),
        compiler_params=_cparams("parallel", "arbitrary"),
        name=name,
    )(a, ss, wt_all)


def _matmul_res(a, w, res, g_next=None, *, tm_pref, tn_pref, name):
    m, k = a.shape
    n = w.shape[1]
    tm, tn = _pick(m, tm_pref), _pick(n, tn_pref)
    in_specs = [pl.BlockSpec((tm, k), lambda i, j: (i, 0)),
                pl.BlockSpec((k, tn), lambda i, j: (0, j)),
                pl.BlockSpec((tm, tn), lambda i, j: (i, j))]
    args = [a, w, res]
    if g_next is None:
        shapes = jax.ShapeDtypeStruct((m, n), jnp.float32)
        specs = pl.BlockSpec((tm, tn), lambda i, j: (i, j))
    else:
        shapes, specs = _norm_out(m, n, tm, tn)
        in_specs.append(pl.BlockSpec((1, tn), lambda i, j: (0, j)))
        args.append(g_next.reshape(1, n))
    return pl.pallas_call(
        functools.partial(_mm_res_kernel, emit_norm=g_next is not None),
        out_shape=shapes,
        grid=(m // tm, n // tn),
        in_specs=in_specs,
        out_specs=specs,
        compiler_params=_cparams("parallel", "arbitrary"),
        name=name,
    )(*args)


def _seg_ones(seg):
    r = lax.broadcasted_iota(jnp.int32, (LANES, LANES), 0) // seg
    c = lax.broadcasted_iota(jnp.int32, (LANES, LANES), 1) // seg
    return (r == c).astype(MXU_DTYPE)


def _seg_rms(x, ones, seg):
    sq = x * x
    hi = sq.astype(MXU_DTYPE)
    lo = (sq - hi.astype(jnp.float32)).astype(MXU_DTYPE)
    ss = _dot(hi, ones) + _dot(lo, ones)
    return x * lax.rsqrt(ss * (1.0 / seg) + EPS)


def _prep_kernel(p_ref, gaq_ref, gak_ref, gbq_ref, gbk_ref,
                 qa2_ref, ka_ref, va_ref, qb_ref, kb_ref, vb_ref, qi2_ref, ki2_ref):
    ones64 = _seg_ones(A_QK_DIM)
    ones128 = _seg_ones(B_HEAD_DIM)
    lane = lax.broadcasted_iota(jnp.int32, (1, LANES), 1)
    lo_half = lane < A_QK_DIM
    gaq, gak, gbq, gbk = gaq_ref[...], gak_ref[...], gbq_ref[...], gbk_ref[...]
    a_scale = A_QK_DIM ** -0.5 * LOG2E
    b_scale = B_HEAD_DIM ** -0.5 * LOG2E
    i_scale = IDX_DIM ** -0.5

    for h in range(A_HEADS):
        q = _seg_rms(p_ref[0, :, OFF_AQ + h * LANES:OFF_AQ + (h + 1) * LANES], ones64, A_QK_DIM)
        q = q * gaq * a_scale
        qa2_ref[0, h, 0] = jnp.where(lo_half, q, 0.0).astype(qa2_ref.dtype)
        qa2_ref[0, h, 1] = jnp.where(lo_half, 0.0, q).astype(qa2_ref.dtype)
        k = _seg_rms(p_ref[0, :, OFF_AK + h * LANES:OFF_AK + (h + 1) * LANES], ones64, A_QK_DIM)
        ka_ref[0, h] = (k * gak).astype(ka_ref.dtype)
        va_ref[0, h] = p_ref[0, :, OFF_AV + h * LANES:OFF_AV + (h + 1) * LANES].astype(va_ref.dtype)
    for h in range(B_HEADS):
        q = _seg_rms(p_ref[0, :, OFF_BQ + h * LANES:OFF_BQ + (h + 1) * LANES], ones128, B_HEAD_DIM)
        qb_ref[0, h] = (q * gbq * b_scale).astype(qb_ref.dtype)
    k = _seg_rms(p_ref[0, :, OFF_BK:OFF_BK + B_K], ones128, B_HEAD_DIM)
    kb_ref[0] = (k * gbk).astype(kb_ref.dtype)
    vb_ref[0] = p_ref[0, :, OFF_BV:OFF_BV + B_V].astype(vb_ref.dtype)
    for hp in range(IDX_HEADS // 2):
        qi = p_ref[0, :, OFF_IQ + hp * LANES:OFF_IQ + (hp + 1) * LANES] * i_scale
        qi2_ref[0, 2 * hp] = jnp.where(lo_half, qi, 0.0).astype(qi2_ref.dtype)
        qi2_ref[0, 2 * hp + 1] = jnp.where(lo_half, 0.0, qi).astype(qi2_ref.dtype)
    kt = p_ref[0, :, OFF_IK:OFF_IK + I_K]
    ki2_ref[0] = jnp.concatenate([kt, kt], axis=-1).astype(ki2_ref.dtype)


def _prep(proj3, gaq, gak, gbq, gbk):
    b, s, npad = proj3.shape
    ts = _pick(s, 256)
    dt = MXU_DTYPE
    head4 = lambda nh: jax.ShapeDtypeStruct((b, nh, s, LANES), dt)
    flat3 = jax.ShapeDtypeStruct((b, s, LANES), dt)
    hspec = lambda nh: pl.BlockSpec((1, nh, ts, LANES), lambda bi, i: (bi, 0, i, 0))
    fspec = pl.BlockSpec((1, ts, LANES), lambda bi, i: (bi, i, 0))
    gspec = pl.BlockSpec((1, LANES), lambda bi, i: (0, 0))
    return pl.pallas_call(
        _prep_kernel,
        out_shape=(jax.ShapeDtypeStruct((b, A_HEADS, 2, s, LANES), dt),
                   head4(A_HEADS), head4(A_HEADS), head4(B_HEADS), flat3, flat3,
                   head4(IDX_HEADS), flat3),
        grid=(b, s // ts),
        in_specs=[pl.BlockSpec((1, ts, npad), lambda bi, i: (bi, i, 0)), gspec, gspec, gspec, gspec],
        out_specs=(pl.BlockSpec((1, A_HEADS, 2, ts, LANES), lambda bi, i: (bi, 0, 0, i, 0)),
                   hspec(A_HEADS), hspec(A_HEADS), hspec(B_HEADS), fspec, fspec,
                   hspec(IDX_HEADS), fspec),
        compiler_params=_cparams("parallel", "parallel"),
        name="head_prep",
    )(proj3, gaq, gak, gbq, gbk)


def _rel_bucket_np(rel):
    nb = N_BUCKETS // 2
    max_exact = nb // 2
    bucket = np.where(rel > 0, nb, 0)
    n = np.abs(rel)
    nf = np.maximum(n, 1).astype(np.float32)
    large = max_exact + (np.log(nf / np.float32(max_exact)) / np.float32(math.log(MAX_DISTANCE / max_exact))
                         * np.float32(nb - max_exact)).astype(np.int32)
    large = np.minimum(large, nb - 1)
    return (bucket + np.where(n < max_exact, n, large)).astype(np.int32)


def _near_bucket_ids(t):
    tq = np.arange(t)[:, None]
    out = []
    for key0 in (0, -t):
        ts = key0 + np.arange(2 * t)[None, :]
        ids = _rel_bucket_np(ts - tq)
        vis = np.floor_divide(ts, CHUNK) <= np.floor_divide(tq, CHUNK)
        out.append(np.where(vis, ids, -1))
    return np.stack(out).astype(np.int32)


FAR_BUCKET = N_BUCKETS // 2 - 1


def _bias_kernel(tab_ref, ids_ref, o_ref):
    h = pl.program_id(0)
    ids = ids_ref[...]
    far = tab_ref[FAR_BUCKET, h]
    acc = jnp.full(ids.shape, NEG_INF, jnp.float32)
    for bkt in range(N_BUCKETS):
        acc = jnp.where(ids == bkt, (tab_ref[bkt, h] - far) * LOG2E, acc)
    o_ref[0] = acc


def _near_bias(rel_bias, t):
    ids = jnp.asarray(_near_bucket_ids(t))
    nh = rel_bias.shape[1]
    return pl.pallas_call(
        _bias_kernel,
        out_shape=jax.ShapeDtypeStruct((nh, 2, t, 2 * t), jnp.float32),
        grid=(nh,),
        in_specs=[pl.BlockSpec(memory_space=pltpu.SMEM),
                  pl.BlockSpec((2, t, 2 * t), lambda h: (0, 0, 0))],
        out_specs=pl.BlockSpec((1, 2, t, 2 * t), lambda h: (h, 0, 0, 0)),
        compiler_params=_cparams("arbitrary"),
        name="near_bias",
    )(rel_bias, ids)


def _softmax_step(s, v, m_sc, l_sc, acc_sc, idx, first):
    rows, width = s.shape
    blocks = [s[:, c * LANES:(c + 1) * LANES] for c in range(width // LANES)]
    m_part = functools.reduce(jnp.maximum, blocks)
    m_cur = jnp.broadcast_to(jnp.max(m_part, axis=-1, keepdims=True), (rows, LANES))
    if first:
        m_new = jnp.maximum(m_cur, M_FLOOR)
    else:
        m_prev = m_sc[idx]
        m_new = jnp.maximum(m_prev, m_cur)
        alpha = jnp.exp2(m_prev - m_new)
    ps = [jnp.exp2(blk - m_new) for blk in blocks]
    l_part = functools.reduce(jnp.add, ps)
    l_cur = jnp.broadcast_to(jnp.sum(l_part, axis=-1, keepdims=True), (rows, LANES))
    p = jnp.concatenate([x.astype(v.dtype) for x in ps], axis=-1)
    pv = _dot(p, v)
    if first:
        l_sc[idx] = l_cur
        acc_sc[idx] = pv
    else:
        l_sc[idx] = alpha * l_sc[idx] + l_cur
        acc_sc[idx] = alpha * acc_sc[idx] + pv
    m_sc[idx] = m_new


def _diff_kernel(q2_ref, k_ref, v_ref, bias_ref, lam_ref, g_ref, o_ref, m_sc, l_sc, acc_sc,
                 *, lam_init, t, hg):
    i = pl.program_id(2)
    n_far = jnp.maximum(i - 1, 0)
    near0 = pl.multiple_of(n_far * t, t)

    def step(k0, width, first):
        for hh in range(hg):
            q2 = q2_ref[0, hh].reshape(2 * t, LANES)
            s = _dot_nt(q2, k_ref[0, hh, pl.ds(k0, width), :])
            if first:
                s = (s.reshape(2, t, width) + bias_ref[hh, 0][None]).reshape(2 * t, width)
            _softmax_step(s, v_ref[0, hh, pl.ds(k0, width), :], m_sc, l_sc, acc_sc, hh, first)

    step(near0, 2 * t, True)

    def far_body(j, carry):
        step(pl.multiple_of(j * 2 * t, 2 * t), 2 * t, False)
        return carry

    lax.fori_loop(0, n_far // 2, far_body, 0)

    @pl.when(n_far % 2 == 1)
    def _():
        step(pl.multiple_of((n_far - 1) * t, t), t, False)

    lq = lam_ref[...]
    lam = (jnp.exp(jnp.sum(lq[0:1] * lq[1:2], axis=-1, keepdims=True))
           - jnp.exp(jnp.sum(lq[2:3] * lq[3:4], axis=-1, keepdims=True)) + lam_init)
    for hh in range(hg):
        o = acc_sc[hh] / l_sc[hh]
        o = o[:t] - lam * o[t:]
        ms = jnp.mean(o * o, axis=-1, keepdims=True)
        o = o * lax.rsqrt(ms + EPS) * g_ref[...]
        o_ref[:, hh * A_V_DIM:(hh + 1) * A_V_DIM] = (o * (1.0 - lam_init)).astype(o_ref.dtype)


def _diff_attention(qa2, ka, va, bias, lam_qk, g_out, lam_init):
    b, nh, _, s, _ = qa2.shape
    t = DIFF_T
    hg = DIFF_HEADS_PER_STEP
    nt = s // t
    rows = 2 * t
    return pl.pallas_call(
        functools.partial(_diff_kernel, lam_init=lam_init, t=t, hg=hg),
        out_shape=jax.ShapeDtypeStruct((b * s, nh * A_V_DIM), MXU_DTYPE),
        grid=(b, nh // hg, nt),
        in_specs=[pl.BlockSpec((1, hg, 2, t, LANES), lambda bi, h, i: (bi, h, 0, i, 0)),
                  pl.BlockSpec((1, hg, s, LANES), lambda bi, h, i: (bi, h, 0, 0)),
                  pl.BlockSpec((1, hg, s, LANES), lambda bi, h, i: (bi, h, 0, 0)),
                  pl.BlockSpec((hg, 1, t, 2 * t), lambda bi, h, i: (h, jnp.minimum(i, 1), 0, 0)),
                  pl.BlockSpec((4, A_QK_DIM), lambda bi, h, i: (0, 0)),
                  pl.BlockSpec((1, A_V_DIM), lambda bi, h, i: (0, 0))],
        out_specs=pl.BlockSpec((t, hg * A_V_DIM), lambda bi, h, i: (bi * nt + i, h)),
        scratch_shapes=[pltpu.VMEM((hg, rows, LANES), jnp.float32),
                        pltpu.VMEM((hg, rows, LANES), jnp.float32),
                        pltpu.VMEM((hg, rows, A_V_DIM), jnp.float32)],
        compiler_params=_cparams("parallel", "parallel", "arbitrary"),
        name="diff_attention",
    )(qa2, ka, va, bias, lam_qk, g_out.reshape(1, A_V_DIM))


def _sortable_key(x):
    bits = pltpu.bitcast(x, jnp.int32)
    return bits ^ ((bits >> 31) & 0x7FFFFFFF)


def _dsa_kernel(qi2_ref, ki2_ref, wi_ref, qb_ref, kb_ref, vb_ref, bias_ref, o_ref,
                key_sc, am_sc, m_sc, l_sc, acc_sc, *, topk, heads_per_group):
    t = ATT_T
    kb_sz = IDX_KB
    i = pl.program_id(1)
    q0 = i * t
    n_idx_blocks = (jnp.maximum(q0 + t, 2 * t) + kb_sz - 1) // kb_sz
    t_idx = q0 + lax.broadcasted_iota(jnp.int32, (1, t), 1)
    t_chunk = t_idx >> CHUNK_SHIFT

    wi = wi_ref[0] * (IDX_HEADS ** -0.5)

    def score_body(kbi, carry):
        k0 = pl.multiple_of(kbi * kb_sz, kb_sz)
        kblk = ki2_ref[0, pl.ds(k0, kb_sz), :]
        sc = jnp.zeros((kb_sz, t), jnp.float32)
        for h in range(IDX_HEADS):
            d = _dot_nt(kblk, qi2_ref[0, h])
            sc = sc + jnp.maximum(d, 0.0) * wi[h:h + 1, :]
        s_chunk = (k0 + lax.broadcasted_iota(jnp.int32, (kb_sz, 1), 0)) >> CHUNK_SHIFT
        sc = jnp.where(s_chunk <= t_chunk, sc, NEG_INF)
        key_sc[pl.ds(k0, kb_sz), :] = _sortable_key(sc)
        return carry

    lax.fori_loop(0, n_idx_blocks, score_body, 0)

    kk = jnp.minimum((t_chunk + 1) * CHUNK, topk)

    def count_ge(cand):
        def body(kbi, c):
            k0 = pl.multiple_of(kbi * kb_sz, kb_sz)
            ge = (key_sc[pl.ds(k0, kb_sz), :] >= cand).astype(jnp.int32)
            return c + jnp.sum(ge.reshape(kb_sz // 8, 8, t), axis=0)
        c8 = lax.fori_loop(0, n_idx_blocks, body, jnp.zeros((8, t), jnp.int32))
        return jnp.sum(c8, axis=0, keepdims=True)

    zero = jnp.zeros((1, t), jnp.int32)
    thr0 = jnp.where(count_ge(zero) >= kk, zero, jnp.full((1, t), INT_MIN, jnp.int32))

    def bit_body(it, thr):
        cand = thr + jnp.left_shift(jnp.int32(1), 30 - it)
        return jnp.where(count_ge(cand) >= kk, cand, thr)

    thr = lax.fori_loop(0, 31, bit_body, thr0)

    def mask_body(kbi, carry):
        k0 = pl.multiple_of(kbi * kb_sz, kb_sz)
        sel = jnp.where(key_sc[pl.ds(k0, kb_sz), :] >= thr, 0.0, NEG_INF)
        am_sc[:, pl.ds(k0, kb_sz)] = sel.T
        return carry

    lax.fori_loop(0, n_idx_blocks, mask_body, 0)

    g = heads_per_group
    n_groups = B_HEADS // g
    far_w = DSA_FAR_W
    near0 = pl.multiple_of(jnp.maximum(i - 1, 0) * t, t)
    am_near = am_sc[:, pl.ds(near0, 2 * t)]
    am_sc[:, pl.ds(near0, far_w + 2 * t)] = jnp.full((t, far_w + 2 * t), NEG_INF, jnp.float32)

    def step(k0, width, first):
        kblk = kb_ref[0, pl.ds(k0, width), :]
        vblk = vb_ref[0, pl.ds(k0, width), :]
        for hg in range(n_groups):
            q = qb_ref[0, hg * g:(hg + 1) * g].reshape(g * t, LANES)
            s = _dot_nt(q, kblk).reshape(g, t, width)
            if first:
                s = s + (bias_ref[hg * g:(hg + 1) * g, 0] + am_near[None])
            else:
                s = s + am_sc[:, pl.ds(k0, width)][None]
            _softmax_step(s.reshape(g * t, width), vblk, m_sc, l_sc, acc_sc, hg, first)

    step(near0, 2 * t, True)

    def far_body(j, carry):
        step(pl.multiple_of(j * far_w, far_w), far_w, False)
        return carry

    lax.fori_loop(0, (near0 + far_w - 1) // far_w, far_body, 0)

    for hg in range(n_groups):
        o = acc_sc[hg] / l_sc[hg]
        for hh in range(g):
            h = hg * g + hh
            o_ref[:, h * B_HEAD_DIM:(h + 1) * B_HEAD_DIM] = o[hh * t:(hh + 1) * t].astype(o_ref.dtype)


def _dsa_attention(qi2, ki2, wi_t, qb, kb, vb, bias, topk):
    b, nh, s, _ = qb.shape
    t = ATT_T
    nt = s // t
    g = DSA_HEADS_PER_GROUP
    n_groups = nh // g
    nb_cols = B_HEADS * B_HEAD_DIM
    return pl.pallas_call(
        functools.partial(_dsa_kernel, topk=topk, heads_per_group=g),
        out_shape=jax.ShapeDtypeStruct((b * s, nb_cols), MXU_DTYPE),
        grid=(b, nt),
        in_specs=[pl.BlockSpec((1, IDX_HEADS, t, LANES), lambda bi, i: (bi, 0, i, 0)),
                  pl.BlockSpec((1, s, LANES), lambda bi, i: (bi, 0, 0)),
                  pl.BlockSpec((1, IDX_HEADS, t), lambda bi, i: (bi, 0, i)),
                  pl.BlockSpec((1, nh, t, LANES), lambda bi, i: (bi, 0, i, 0)),
                  pl.BlockSpec((1, s, LANES), lambda bi, i: (bi, 0, 0)),
                  pl.BlockSpec((1, s, LANES), lambda bi, i: (bi, 0, 0)),
                  pl.BlockSpec((B_HEADS, 1, t, 2 * t), lambda bi, i: (0, jnp.minimum(i, 1), 0, 0))],
        out_specs=pl.BlockSpec((t, nb_cols), lambda bi, i: (bi * nt + i, 0)),
        scratch_shapes=[pltpu.VMEM((s, t), jnp.int32),
                        pltpu.VMEM((t, s + DSA_FAR_W), jnp.float32),
                        pltpu.VMEM((n_groups, g * t, LANES), jnp.float32),
                        pltpu.VMEM((n_groups, g * t, LANES), jnp.float32),
                        pltpu.VMEM((n_groups, g * t, B_HEAD_DIM), jnp.float32)],
        compiler_params=_cparams("parallel", "arbitrary"),
        name="dsa_attention",
    )(qi2, ki2, wi_t, qb, kb, vb, bias)


def _ffn1_kernel(h_ref, ss_ref, hp_ref, ssp_ref, wg_ref, wu_ref, cw_ref, cb_ref, wd_ref,
                 o_ref, wdc_ref, *, tiles_per_seq):
    wdc_ref[...] = wd_ref[...].astype(wdc_ref.dtype)
    i = pl.program_id(0)
    tm, d = h_ref.shape
    h = h_ref[...]
    rs = _row_scale(ss_ref, d)
    wg = wg_ref[...].astype(MXU_DTYPE)
    gate = _dot(h, wg) * rs
    up = _dot(h, wu_ref[...].astype(MXU_DTYPE)) * rs
    prev = _dot(hp_ref[...], wg) * _row_scale(ssp_ref, d)
    prev = jnp.where(i % tiles_per_seq == 0, 0.0, prev)
    row = lax.broadcasted_iota(jnp.int32, (tm, 1), 0)
    g1 = jnp.where(row == 0, prev[7:8], pltpu.roll(gate, 1, 0))
    g2 = pltpu.roll(gate, 2, 0)
    g2 = jnp.where(row == 0, prev[6:7], jnp.where(row == 1, prev[7:8], g2))
    cw = cw_ref[...]
    gc = cw[0:1] * g2 + cw[1:2] * g1 + cw[2:3] * gate + cb_ref[...]
    act = gc * (1.0 / (1.0 + jnp.exp(-gc))) * up
    o_ref[...] = act.astype(o_ref.dtype)


def _ffn1(h, ss, w_gu_all, conv_w_all, conv_b_all, w_down_all, layer, seq):
    m, d = h.shape
    dff = conv_w_all.shape[2]
    d_out = w_down_all.shape[2]
    tm = _pick(seq, 1024)
    tn = _pick(dff, 256)
    nj = dff // tn
    sub = 8
    n_steps = (m // tm) * nj
    slab = dff // n_steps
    assert slab * n_steps == dff and slab % 16 == 0
    return pl.pallas_call(
        functools.partial(_ffn1_kernel, tiles_per_seq=seq // tm),
        out_shape=(jax.ShapeDtypeStruct((m, dff), MXU_DTYPE),
                   jax.ShapeDtypeStruct((dff, d_out), MXU_DTYPE)),
        grid=(m // tm, nj),
        in_specs=[pl.BlockSpec((tm, d), lambda i, j: (i, 0)),
                  pl.BlockSpec((tm, LANES), lambda i, j: (i, 0)),
                  pl.BlockSpec((sub, d), lambda i, j: (jnp.maximum(i * (tm // sub) - 1, 0), 0)),
                  pl.BlockSpec((sub, LANES), lambda i, j: (jnp.maximum(i * (tm // sub) - 1, 0), 0)),
                  pl.BlockSpec((None, d, tn), lambda i, j: (layer, 0, j)),
                  pl.BlockSpec((None, d, tn), lambda i, j: (layer, 0, j + nj)),
                  pl.BlockSpec((None, CONV_WIDTH, tn), lambda i, j: (layer, 0, j)),
                  pl.BlockSpec((None, 1, tn), lambda i, j: (layer, 0, j)),
                  pl.BlockSpec((None, slab, d_out), lambda i, j: (layer, i * nj + j, 0))],
        out_specs=(pl.BlockSpec((tm, tn), lambda i, j: (i, j)),
                   pl.BlockSpec((slab, d_out), lambda i, j: (i * nj + j, 0))),
        compiler_params=_cparams("arbitrary", "arbitrary"),
        name="ffn_gate_up_glu",
    )(h, ss, h, ss, w_gu_all, w_gu_all, conv_w_all, conv_b_all.reshape(-1, 1, dff), w_down_all)


def kernel(x, attn_norm, w_in, a_q_norm, a_k_norm, lambda_qk, a_out_norm, b_q_norm, b_k_norm,
           rel_bias, w_out, ffn_norm, w_gate_up, conv_w, conv_b, w_down):
    bsz, s_len, d_model = x.shape
    depth = w_in.shape[0]
    m = bsz * s_len
    topk = min(TOPK_MAX, s_len // 4)
    assert s_len % (2 * DIFF_T) == 0 and s_len % DSA_FAR_W == 0 and s_len % IDX_KB == 0
    assert w_in.shape[2] == D_IN and w_out.shape[1] == A_V + B_Q

    near_bias_a = _near_bias(rel_bias[:, :A_HEADS], DIFF_T)
    near_bias_b = _near_bias(rel_bias[:, A_HEADS:], ATT_T)
    x2 = x.reshape(m, d_model)
    w_in_t = jnp.swapaxes(w_in, 1, 2)
    xg, ss = _norm_inputs(x2, attn_norm[0])
    for l in range(depth):
        lam_init = 0.8 - 0.6 * math.exp(-0.3 * l)
        proj = _matmul_nt(xg, ss, w_in_t, l, tm_pref=1024, tn_pref=512, name="in_proj")
        proj3 = proj.reshape(bsz, s_len, D_IN)
        two = lambda g: jnp.concatenate([g, g]).reshape(1, LANES)
        qa2, ka, va, qb, kb, vb, qi2, ki2 = _prep(
            proj3, two(a_q_norm[l]), two(a_k_norm[l]),
            b_q_norm[l].reshape(1, LANES), b_k_norm[l].reshape(1, LANES))
        wi_t = jnp.swapaxes(proj3[:, :, OFF_IW:OFF_IW + I_W], 1, 2)
        mix_a = _diff_attention(qa2, ka, va, near_bias_a, lambda_qk[l], a_out_norm[l], lam_init)
        mix_b = _dsa_attention(qi2, ki2, wi_t, qb, kb, vb, near_bias_b, topk)
        x2, xg, ss = _matmul2_res(mix_a, mix_b, w_out, l, x2, ffn_norm[l],
                                  tm_pref=1024, tn_pref=512, name="out_proj")
        act, w_down_c = _ffn1(xg, ss, w_gate_up, conv_w, conv_b, w_down, l, s_len)
        if l + 1 < depth:
            x2, xg, ss = _matmul_res(act, w_down_c, x2, attn_norm[l + 1],
                                     tm_pref=512, tn_pref=256, name="down_proj")
        else:
            x2 = _matmul_res(act, w_down_c, x2, tm_pref=512, tn_pref=256, name="down_proj")
    return x2.reshape(bsz, s_len, d_model)
```

```python
import functools
import math

import numpy as np
import jax
import jax.numpy as jnp
from jax import lax
from jax.experimental import pallas as pl
from jax.experimental.pallas import tpu as pltpu

CHUNK = 64
CHUNK_SHIFT = 6
A_HEADS = 16
A_QK_DIM = 64
A_V_DIM = 128
B_HEADS = 16
B_HEAD_DIM = 128
IDX_HEADS = 16
IDX_DIM = 64
TOPK_MAX = 256
N_BUCKETS = 32
MAX_DISTANCE = 128
CONV_WIDTH = 3
EPS = 1e-6

A_Q = A_HEADS * 2 * A_QK_DIM
A_K = A_HEADS * 2 * A_QK_DIM
A_V = A_HEADS * A_V_DIM
B_Q = B_HEADS * B_HEAD_DIM
B_K = B_HEAD_DIM
B_V = B_HEAD_DIM
I_Q = IDX_HEADS * IDX_DIM
I_K = IDX_DIM
I_W = IDX_HEADS
OFF_AQ = 0
OFF_AK = OFF_AQ + A_Q
OFF_AV = OFF_AK + A_K
OFF_BQ = OFF_AV + A_V
OFF_BK = OFF_BQ + B_Q
OFF_BV = OFF_BK + B_K
OFF_IQ = OFF_BV + B_V
OFF_IK = OFF_IQ + I_Q
OFF_IW = OFF_IK + I_K
D_IN = OFF_IW + I_W

LANES = 128
V7X_VMEM_LIMIT = 56 * 1024 * 1024

DIFF_T = 256
DIFF_HEADS_PER_STEP = 8
ATT_T = 256
IDX_KB = 256
DSA_FAR_W = 512
DSA_HEADS_PER_GROUP = 2
MXU_DTYPE = jnp.bfloat16

LOG2E = math.log2(math.e)
NEG_INF = float("-inf")
M_FLOOR = -1e30
I16_MIN = -(2 ** 15)


def _cparams(*sem):
    return pltpu.CompilerParams(dimension_semantics=sem, vmem_limit_bytes=V7X_VMEM_LIMIT)


def _pick(n, pref):
    t = min(pref, n)
    while n % t:
        t -= LANES
    return t


def _dot(a, b):
    return jnp.dot(a, b, preferred_element_type=jnp.float32)


def _dot_nt(a, b):
    return lax.dot_general(a, b, (((1,), (1,)), ((), ())), preferred_element_type=jnp.float32)


def _row_sumsq(x):
    sq = functools.reduce(jnp.add, [x[:, c * LANES:(c + 1) * LANES] ** 2
                                    for c in range(x.shape[1] // LANES)])
    return jnp.broadcast_to(jnp.sum(sq, axis=-1, keepdims=True), (x.shape[0], LANES))


def _row_scale(ss_ref, d):
    return lax.rsqrt(ss_ref[:, 0:1] * (1.0 / d) + EPS)


def _emit_norm_inputs(x, j, g_ref, xg_ref, ss_ref):
    xg_ref[...] = (x * g_ref[...]).astype(xg_ref.dtype)

    @pl.when(j == 0)
    def _():
        ss_ref[...] = jnp.zeros(ss_ref.shape, jnp.float32)

    ss_ref[...] += _row_sumsq(x)


def _norm_inputs_kernel(x_ref, g_ref, xg_ref, ss_ref):
    x = x_ref[...]
    xg_ref[...] = (x * g_ref[...]).astype(xg_ref.dtype)
    ss_ref[...] = _row_sumsq(x)


def _norm_inputs(x2d, g):
    m, d = x2d.shape
    tm = _pick(m, 256)
    return pl.pallas_call(
        _norm_inputs_kernel,
        out_shape=(jax.ShapeDtypeStruct((m, d), MXU_DTYPE),
                   jax.ShapeDtypeStruct((m, LANES), jnp.float32)),
        grid=(m // tm,),
        in_specs=[pl.BlockSpec((tm, d), lambda i: (i, 0)),
                  pl.BlockSpec((1, d), lambda i: (0, 0))],
        out_specs=(pl.BlockSpec((tm, d), lambda i: (i, 0)),
                   pl.BlockSpec((tm, LANES), lambda i: (i, 0))),
        compiler_params=_cparams("parallel"),
        name="norm_inputs",
    )(x2d, g.reshape(1, d))


def _mm_nt_kernel(a_ref, ss_ref, wt_ref, o_ref):
    acc = _dot_nt(a_ref[...], wt_ref[...].astype(MXU_DTYPE))
    o_ref[...] = (acc * _row_scale(ss_ref, a_ref.shape[1])).astype(o_ref.dtype)


def _mm_res_kernel(a_ref, w_ref, r_ref, *rest, emit_norm):
    x = r_ref[...] + _dot(a_ref[...], w_ref[...])
    if emit_norm:
        g_ref, o_ref, xg_ref, ss_ref = rest
        _emit_norm_inputs(x, pl.program_id(1), g_ref, xg_ref, ss_ref)
    else:
        (o_ref,) = rest
    o_ref[...] = x


def _mm2_res_kernel(a1_ref, a2_ref, w1_ref, w2_ref, r_ref, g_ref, o_ref, xg_ref, ss_ref):
    acc = (_dot(a1_ref[...], w1_ref[...].astype(MXU_DTYPE))
           + _dot(a2_ref[...], w2_ref[...].astype(MXU_DTYPE)))
    x = r_ref[...] + acc
    _emit_norm_inputs(x, pl.program_id(1), g_ref, xg_ref, ss_ref)
    o_ref[...] = x


def _norm_out(m, n, tm, tn):
    shapes = (jax.ShapeDtypeStruct((m, n), jnp.float32),
              jax.ShapeDtypeStruct((m, n), MXU_DTYPE),
              jax.ShapeDtypeStruct((m, LANES), jnp.float32))
    specs = (pl.BlockSpec((tm, tn), lambda i, j: (i, j)),
             pl.BlockSpec((tm, tn), lambda i, j: (i, j)),
             pl.BlockSpec((tm, LANES), lambda i, j: (i, 0)))
    return shapes, specs


def _matmul2_res(a1, a2, w_all, layer, res, g_next, *, tm_pref, tn_pref, name):
    m, k = a1.shape
    n = w_all.shape[2]
    tm, tn = _pick(m, tm_pref), _pick(n, tn_pref)
    shapes, specs = _norm_out(m, n, tm, tn)
    return pl.pallas_call(
        _mm2_res_kernel,
        out_shape=shapes,
        grid=(m // tm, n // tn),
        in_specs=[pl.BlockSpec((tm, k), lambda i, j: (i, 0)),
                  pl.BlockSpec((tm, k), lambda i, j: (i, 0)),
                  pl.BlockSpec((None, k, tn), lambda i, j: (layer, 0, j)),
                  pl.BlockSpec((None, k, tn), lambda i, j: (layer, 1, j)),
                  pl.BlockSpec((tm, tn), lambda i, j: (i, j)),
                  pl.BlockSpec((1, tn), lambda i, j: (0, j))],
        out_specs=specs,
        compiler_params=_cparams("parallel", "arbitrary"),
        name=name,
    )(a1, a2, w_all, w_all, res, g_next.reshape(1, n))


def _matmul_nt(a, ss, wt_all, layer, *, tm_pref, tn_pref, name):
    m, k = a.shape
    n = wt_all.shape[1]
    tm, tn = _pick(m, tm_pref), min(tn_pref, n)
    return pl.pallas_call(
        _mm_nt_kernel,
        out_shape=jax.ShapeDtypeStruct((m, n), jnp.float32),
        grid=(m // tm, pl.cdiv(n, tn)),
        in_specs=[pl.BlockSpec((tm, k), lambda i, j: (i, 0)),
                  pl.BlockSpec((tm, LANES), lambda i, j: (i, 0)),
                  pl.BlockSpec((None, tn, k), lambda i, j: (layer, j, 0))],
        out_specs=pl.BlockSpec((tm, tn), lambda i, j: (i, j)),
        compiler_params=_cparams("parallel", "arbitrary"),
        name=name,
    )(a, ss, wt_all)


def _matmul_res(a, w, res, g_next=None, *, tm_pref, tn_pref, name):
    m, k = a.shape
    n = w.shape[1]
    tm, tn = _pick(m, tm_pref), _pick(n, tn_pref)
    in_specs = [pl.BlockSpec((tm, k), lambda i, j: (i, 0)),
                pl.BlockSpec((k, tn), lambda i, j: (0, j)),
                pl.BlockSpec((tm, tn), lambda i, j: (i, j))]
    args = [a, w, res]
    if g_next is None:
        shapes = jax.ShapeDtypeStruct((m, n), jnp.float32)
        specs = pl.BlockSpec((tm, tn), lambda i, j: (i, j))
    else:
        shapes, specs = _norm_out(m, n, tm, tn)
        in_specs.append(pl.BlockSpec((1, tn), lambda i, j: (0, j)))
        args.append(g_next.reshape(1, n))
    return pl.pallas_call(
        functools.partial(_mm_res_kernel, emit_norm=g_next is not None),
        out_shape=shapes,
        grid=(m // tm, n // tn),
        in_specs=in_specs,
        out_specs=specs,
        compiler_params=_cparams("parallel", "arbitrary"),
        name=name,
    )(*args)


def _seg_ones(seg):
    r = lax.broadcasted_iota(jnp.int32, (LANES, LANES), 0) // seg
    c = lax.broadcasted_iota(jnp.int32, (LANES, LANES), 1) // seg
    return (r == c).astype(MXU_DTYPE)


def _seg_rms(x, ones, seg):
    sq = x * x
    hi = sq.astype(MXU_DTYPE)
    lo = (sq - hi.astype(jnp.float32)).astype(MXU_DTYPE)
    ss = _dot(hi, ones) + _dot(lo, ones)
    return x * lax.rsqrt(ss * (1.0 / seg) + EPS)


def _prep_kernel(p_ref, gaq_ref, gak_ref, gbq_ref, gbk_ref,
                 qa2_ref, ka_ref, va_ref, qb_ref, kb_ref, vb_ref, qi2_ref, ki2_ref):
    ones64 = _seg_ones(A_QK_DIM)
    ones128 = _seg_ones(B_HEAD_DIM)
    lane = lax.broadcasted_iota(jnp.int32, (1, LANES), 1)
    lo_half = lane < A_QK_DIM
    gaq, gak, gbq, gbk = gaq_ref[...], gak_ref[...], gbq_ref[...], gbk_ref[...]
    a_scale = A_QK_DIM ** -0.5 * LOG2E
    b_scale = B_HEAD_DIM ** -0.5 * LOG2E
    i_scale = IDX_DIM ** -0.5

    for h in range(A_HEADS):
        q = _seg_rms(p_ref[0, :, OFF_AQ + h * LANES:OFF_AQ + (h + 1) * LANES], ones64, A_QK_DIM)
        q = q * gaq * a_scale
        qa2_ref[0, h, 0] = jnp.where(lo_half, q, 0.0).astype(qa2_ref.dtype)
        qa2_ref[0, h, 1] = jnp.where(lo_half, 0.0, q).astype(qa2_ref.dtype)
        k = _seg_rms(p_ref[0, :, OFF_AK + h * LANES:OFF_AK + (h + 1) * LANES], ones64, A_QK_DIM)
        ka_ref[0, h] = (k * gak).astype(ka_ref.dtype)
        va_ref[0, h] = p_ref[0, :, OFF_AV + h * LANES:OFF_AV + (h + 1) * LANES].astype(va_ref.dtype)
    for h in range(B_HEADS):
        q = _seg_rms(p_ref[0, :, OFF_BQ + h * LANES:OFF_BQ + (h + 1) * LANES], ones128, B_HEAD_DIM)
        qb_ref[0, h] = (q * gbq * b_scale).astype(qb_ref.dtype)
    k = _seg_rms(p_ref[0, :, OFF_BK:OFF_BK + B_K], ones128, B_HEAD_DIM)
    kb_ref[0] = (k * gbk).astype(kb_ref.dtype)
    vb_ref[0] = p_ref[0, :, OFF_BV:OFF_BV + B_V].astype(vb_ref.dtype)
    for hp in range(IDX_HEADS // 2):
        qi = p_ref[0, :, OFF_IQ + hp * LANES:OFF_IQ + (hp + 1) * LANES] * i_scale
        qi2_ref[0, 2 * hp] = jnp.where(lo_half, qi, 0.0).astype(qi2_ref.dtype)
        qi2_ref[0, 2 * hp + 1] = jnp.where(lo_half, 0.0, qi).astype(qi2_ref.dtype)
    kt = p_ref[0, :, OFF_IK:OFF_IK + I_K]
    ki2_ref[0] = jnp.concatenate([kt, kt], axis=-1).astype(ki2_ref.dtype)


def _prep(proj3, gaq, gak, gbq, gbk):
    b, s, npad = proj3.shape
    ts = _pick(s, 256)
    dt = MXU_DTYPE
    head4 = lambda nh: jax.ShapeDtypeStruct((b, nh, s, LANES), dt)
    flat3 = jax.ShapeDtypeStruct((b, s, LANES), dt)
    hspec = lambda nh: pl.BlockSpec((1, nh, ts, LANES), lambda bi, i: (bi, 0, i, 0))
    fspec = pl.BlockSpec((1, ts, LANES), lambda bi, i: (bi, i, 0))
    gspec = pl.BlockSpec((1, LANES), lambda bi, i: (0, 0))
    return pl.pallas_call(
        _prep_kernel,
        out_shape=(jax.ShapeDtypeStruct((b, A_HEADS, 2, s, LANES), dt),
                   head4(A_HEADS), head4(A_HEADS), head4(B_HEADS), flat3, flat3,
                   head4(IDX_HEADS), flat3),
        grid=(b, s // ts),
        in_specs=[pl.BlockSpec((1, ts, npad), lambda bi, i: (bi, i, 0)), gspec, gspec, gspec, gspec],
        out_specs=(pl.BlockSpec((1, A_HEADS, 2, ts, LANES), lambda bi, i: (bi, 0, 0, i, 0)),
                   hspec(A_HEADS), hspec(A_HEADS), hspec(B_HEADS), fspec, fspec,
                   hspec(IDX_HEADS), fspec),
        compiler_params=_cparams("parallel", "parallel"),
        name="head_prep",
    )(proj3, gaq, gak, gbq, gbk)


def _rel_bucket_np(rel):
    nb = N_BUCKETS // 2
    max_exact = nb // 2
    bucket = np.where(rel > 0, nb, 0)
    n = np.abs(rel)
    nf = np.maximum(n, 1).astype(np.float32)
    large = max_exact + (np.log(nf / np.float32(max_exact)) / np.float32(math.log(MAX_DISTANCE / max_exact))
                         * np.float32(nb - max_exact)).astype(np.int32)
    large = np.minimum(large, nb - 1)
    return (bucket + np.where(n < max_exact, n, large)).astype(np.int32)


def _near_bucket_ids(t):
    tq = np.arange(t)[:, None]
    out = []
    for key0 in (0, -t):
        ts = key0 + np.arange(2 * t)[None, :]
        ids = _rel_bucket_np(ts - tq)
        vis = np.floor_divide(ts, CHUNK) <= np.floor_divide(tq, CHUNK)
        out.append(np.where(vis, ids, -1))
    return np.stack(out).astype(np.int32)


FAR_BUCKET = N_BUCKETS // 2 - 1


def _bias_kernel(tab_ref, ids_ref, o_ref):
    h = pl.program_id(0)
    ids = ids_ref[...]
    far = tab_ref[FAR_BUCKET, h]
    acc = jnp.full(ids.shape, NEG_INF, jnp.float32)
    for bkt in range(N_BUCKETS):
        acc = jnp.where(ids == bkt, (tab_ref[bkt, h] - far) * LOG2E, acc)
    o_ref[0] = acc


def _near_bias(rel_bias, t):
    ids = jnp.asarray(_near_bucket_ids(t))
    nh = rel_bias.shape[1]
    return pl.pallas_call(
        _bias_kernel,
        out_shape=jax.ShapeDtypeStruct((nh, 2, t, 2 * t), jnp.float32),
        grid=(nh,),
        in_specs=[pl.BlockSpec(memory_space=pltpu.SMEM),
                  pl.BlockSpec((2, t, 2 * t), lambda h: (0, 0, 0))],
        out_specs=pl.BlockSpec((1, 2, t, 2 * t), lambda h: (h, 0, 0, 0)),
        compiler_params=_cparams("arbitrary"),
        name="near_bias",
    )(rel_bias, ids)


def _softmax_step(s, v, m_sc, l_sc, acc_sc, idx, first):
    rows, width = s.shape
    blocks = [s[:, c * LANES:(c + 1) * LANES] for c in range(width // LANES)]
    m_part = functools.reduce(jnp.maximum, blocks)
    m_cur = jnp.broadcast_to(jnp.max(m_part, axis=-1, keepdims=True), (rows, LANES))
    if first:
        m_new = jnp.maximum(m_cur, M_FLOOR)
    else:
        m_prev = m_sc[idx]
        m_new = jnp.maximum(m_prev, m_cur)
        alpha = jnp.exp2(m_prev - m_new)
    ps = [jnp.exp2(blk - m_new) for blk in blocks]
    l_part = functools.reduce(jnp.add, ps)
    l_cur = jnp.broadcast_to(jnp.sum(l_part, axis=-1, keepdims=True), (rows, LANES))
    p = jnp.concatenate([x.astype(v.dtype) for x in ps], axis=-1)
    pv = _dot(p, v)
    if first:
        l_sc[idx] = l_cur
        acc_sc[idx] = pv
    else:
        l_sc[idx] = alpha * l_sc[idx] + l_cur
        acc_sc[idx] = alpha * acc_sc[idx] + pv
    m_sc[idx] = m_new


def _diff_kernel(q2_ref, k_ref, v_ref, bias_ref, lam_ref, g_ref, o_ref, m_sc, l_sc, acc_sc,
                 *, lam_init, t, hg):
    i = pl.program_id(2)
    n_far = jnp.maximum(i - 1, 0)
    near0 = pl.multiple_of(n_far * t, t)

    def step(k0, width, first):
        for hh in range(hg):
            q2 = q2_ref[0, hh].reshape(2 * t, LANES)
            s = _dot_nt(q2, k_ref[0, hh, pl.ds(k0, width), :])
            if first:
                s = (s.reshape(2, t, width) + bias_ref[hh, 0][None]).reshape(2 * t, width)
            _softmax_step(s, v_ref[0, hh, pl.ds(k0, width), :], m_sc, l_sc, acc_sc, hh, first)

    step(near0, 2 * t, True)

    def far_body(j, carry):
        step(pl.multiple_of(j * 2 * t, 2 * t), 2 * t, False)
        return carry

    lax.fori_loop(0, n_far // 2, far_body, 0)

    @pl.when(n_far % 2 == 1)
    def _():
        step(pl.multiple_of((n_far - 1) * t, t), t, False)

    lq = lam_ref[...]
    lam = (jnp.exp(jnp.sum(lq[0:1] * lq[1:2], axis=-1, keepdims=True))
           - jnp.exp(jnp.sum(lq[2:3] * lq[3:4], axis=-1, keepdims=True)) + lam_init)
    for hh in range(hg):
        o = acc_sc[hh] / l_sc[hh]
        o = o[:t] - lam * o[t:]
        ms = jnp.mean(o * o, axis=-1, keepdims=True)
        o = o * lax.rsqrt(ms + EPS) * g_ref[...]
        o_ref[:, hh * A_V_DIM:(hh + 1) * A_V_DIM] = (o * (1.0 - lam_init)).astype(o_ref.dtype)


def _diff_attention(qa2, ka, va, bias, lam_qk, g_out, lam_init):
    b, nh, _, s, _ = qa2.shape
    t = DIFF_T
    hg = DIFF_HEADS_PER_STEP
    nt = s // t
    rows = 2 * t
    return pl.pallas_call(
        functools.partial(_diff_kernel, lam_init=lam_init, t=t, hg=hg),
        out_shape=jax.ShapeDtypeStruct((b * s, nh * A_V_DIM), MXU_DTYPE),
        grid=(b, nh // hg, nt),
        in_specs=[pl.BlockSpec((1, hg, 2, t, LANES), lambda bi, h, i: (bi, h, 0, i, 0)),
                  pl.BlockSpec((1, hg, s, LANES), lambda bi, h, i: (bi, h, 0, 0)),
                  pl.BlockSpec((1, hg, s, LANES), lambda bi, h, i: (bi, h, 0, 0)),
                  pl.BlockSpec((hg, 1, t, 2 * t), lambda bi, h, i: (h, jnp.minimum(i, 1), 0, 0)),
                  pl.BlockSpec((4, A_QK_DIM), lambda bi, h, i: (0, 0)),
                  pl.BlockSpec((1, A_V_DIM), lambda bi, h, i: (0, 0))],
        out_specs=pl.BlockSpec((t, hg * A_V_DIM), lambda bi, h, i: (bi * nt + i, h)),
        scratch_shapes=[pltpu.VMEM((hg, rows, LANES), jnp.float32),
                        pltpu.VMEM((hg, rows, LANES), jnp.float32),
                        pltpu.VMEM((hg, rows, A_V_DIM), jnp.float32)],
        compiler_params=_cparams("parallel", "parallel", "arbitrary"),
        name="diff_attention",
    )(qa2, ka, va, bias, lam_qk, g_out.reshape(1, A_V_DIM))


def _sortable_key(x):
    bits = pltpu.bitcast(x, jnp.int32)
    return bits ^ ((bits >> 31) & 0x7FFFFFFF)


def _dsa_kernel(qi2_ref, ki2_ref, wi_ref, qb_ref, kb_ref, vb_ref, bias_ref, o_ref,
                key_sc, hi_sc, lo_sc, am_sc, m_sc, l_sc, acc_sc, *, topk, heads_per_group):
    t = ATT_T
    kb_sz = IDX_KB
    i = pl.program_id(1)
    q0 = i * t
    n_idx_blocks = (jnp.maximum(q0 + t, 2 * t) + kb_sz - 1) // kb_sz
    t_idx = q0 + lax.broadcasted_iota(jnp.int32, (1, t), 1)
    t_chunk = t_idx >> CHUNK_SHIFT

    wi = wi_ref[0] * (IDX_HEADS ** -0.5)

    def score_body(kbi, carry):
        k0 = pl.multiple_of(kbi * kb_sz, kb_sz)
        kblk = ki2_ref[0, pl.ds(k0, kb_sz), :]
        sc = jnp.zeros((kb_sz, t), jnp.float32)
        for h in range(IDX_HEADS):
            d = _dot_nt(kblk, qi2_ref[0, h])
            sc = sc + jnp.maximum(d, 0.0) * wi[h:h + 1, :]
        s_chunk = (k0 + lax.broadcasted_iota(jnp.int32, (kb_sz, 1), 0)) >> CHUNK_SHIFT
        sc = jnp.where(s_chunk <= t_chunk, sc, NEG_INF)
        key = _sortable_key(sc)
        key_sc[pl.ds(k0, kb_sz), :] = key
        hi_sc[pl.ds(k0, kb_sz), :] = (key >> 16).astype(jnp.int16)
        lo_sc[pl.ds(k0, kb_sz), :] = ((key & 0xFFFF) + I16_MIN).astype(jnp.int16)
        return carry

    lax.fori_loop(0, n_idx_blocks, score_body, 0)

    kk = jnp.minimum((t_chunk + 1) * CHUNK, topk)

    def count16(ref, cand, strict):
        cand16 = cand.astype(jnp.int16)

        def body(kbi, c):
            k0 = pl.multiple_of(kbi * kb_sz, kb_sz)
            blk = ref[pl.ds(k0, kb_sz), :]
            hit = (blk > cand16) if strict else (blk >= cand16)
            h3 = hit.astype(jnp.int16).reshape(kb_sz // 16, 16, t)
            for r in range(kb_sz // 16):
                c = c + h3[r]
            return c

        c16 = lax.fori_loop(0, n_idx_blocks, body, jnp.zeros((16, t), jnp.int16))
        return jnp.sum(c16.astype(jnp.int32), axis=0, keepdims=True)

    def search16(ref, base):
        def bit_body(it, thr):
            cand = thr + jnp.left_shift(jnp.int32(1), 15 - it)
            return jnp.where(base + count16(ref, cand, False) >= kk, cand, thr)
        return lax.fori_loop(0, 16, bit_body, jnp.full((1, t), I16_MIN, jnp.int32))

    thr_hi = search16(hi_sc, jnp.zeros((1, t), jnp.int32))
    above = count16(hi_sc, thr_hi, True)
    thr_hi16 = thr_hi.astype(jnp.int16)

    def tie_body(kbi, carry):
        k0 = pl.multiple_of(kbi * kb_sz, kb_sz)
        same = hi_sc[pl.ds(k0, kb_sz), :] == thr_hi16
        lo_sc[pl.ds(k0, kb_sz), :] = jnp.where(same, lo_sc[pl.ds(k0, kb_sz), :], jnp.int16(I16_MIN))
        return carry

    lax.fori_loop(0, n_idx_blocks, tie_body, 0)
    thr_lo = search16(lo_sc, above)
    thr = (thr_hi << 16) + (thr_lo - I16_MIN)

    def mask_body(kbi, carry):
        k0 = pl.multiple_of(kbi * kb_sz, kb_sz)
        sel = jnp.where(key_sc[pl.ds(k0, kb_sz), :] >= thr, 0.0, NEG_INF)
        am_sc[:, pl.ds(k0, kb_sz)] = sel.T
        return carry

    lax.fori_loop(0, n_idx_blocks, mask_body, 0)

    g = heads_per_group
    n_groups = B_HEADS // g
    far_w = DSA_FAR_W
    near0 = pl.multiple_of(jnp.maximum(i - 1, 0) * t, t)
    am_near = am_sc[:, pl.ds(near0, 2 * t)]
    am_sc[:, pl.ds(near0, far_w + 2 * t)] = jnp.full((t, far_w + 2 * t), NEG_INF, jnp.float32)

    def step(k0, width, first):
        kblk = kb_ref[0, pl.ds(k0, width), :]
        vblk = vb_ref[0, pl.ds(k0, width), :]
        for hg in range(n_groups):
            q = qb_ref[0, hg * g:(hg + 1) * g].reshape(g * t, LANES)
            s = _dot_nt(q, kblk).reshape(g, t, width)
            if first:
                s = s + (bias_ref[hg * g:(hg + 1) * g, 0] + am_near[None])
            else:
                s = s + am_sc[:, pl.ds(k0, width)][None]
            _softmax_step(s.reshape(g * t, width), vblk, m_sc, l_sc, acc_sc, hg, first)

    step(near0, 2 * t, True)

    def far_body(j, carry):
        step(pl.multiple_of(j * far_w, far_w), far_w, False)
        return carry

    lax.fori_loop(0, (near0 + far_w - 1) // far_w, far_body, 0)

    for hg in range(n_groups):
        o = acc_sc[hg] / l_sc[hg]
        for hh in range(g):
            h = hg * g + hh
            o_ref[:, h * B_HEAD_DIM:(h + 1) * B_HEAD_DIM] = o[hh * t:(hh + 1) * t].astype(o_ref.dtype)


def _dsa_attention(qi2, ki2, wi_t, qb, kb, vb, bias, topk):
    b, nh, s, _ = qb.shape
    t = ATT_T
    nt = s // t
    g = DSA_HEADS_PER_GROUP
    n_groups = nh // g
    nb_cols = B_HEADS * B_HEAD_DIM
    return pl.pallas_call(
        functools.partial(_dsa_kernel, topk=topk, heads_per_group=g),
        out_shape=jax.ShapeDtypeStruct((b * s, nb_cols), MXU_DTYPE),
        grid=(b, nt),
        in_specs=[pl.BlockSpec((1, IDX_HEADS, t, LANES), lambda bi, i: (bi, 0, i, 0)),
                  pl.BlockSpec((1, s, LANES), lambda bi, i: (bi, 0, 0)),
                  pl.BlockSpec((1, IDX_HEADS, t), lambda bi, i: (bi, 0, i)),
                  pl.BlockSpec((1, nh, t, LANES), lambda bi, i: (bi, 0, i, 0)),
                  pl.BlockSpec((1, s, LANES), lambda bi, i: (bi, 0, 0)),
                  pl.BlockSpec((1, s, LANES), lambda bi, i: (bi, 0, 0)),
                  pl.BlockSpec((B_HEADS, 1, t, 2 * t), lambda bi, i: (0, jnp.minimum(i, 1), 0, 0))],
        out_specs=pl.BlockSpec((t, nb_cols), lambda bi, i: (bi * nt + i, 0)),
        scratch_shapes=[pltpu.VMEM((s, t), jnp.int32),
                        pltpu.VMEM((s, t), jnp.int16),
                        pltpu.VMEM((s, t), jnp.int16),
                        pltpu.VMEM((t, s + DSA_FAR_W), jnp.float32),
                        pltpu.VMEM((n_groups, g * t, LANES), jnp.float32),
                        pltpu.VMEM((n_groups, g * t, LANES), jnp.float32),
                        pltpu.VMEM((n_groups, g * t, B_HEAD_DIM), jnp.float32)],
        compiler_params=_cparams("parallel", "arbitrary"),
        name="dsa_attention",
    )(qi2, ki2, wi_t, qb, kb, vb, bias)


def _ffn1_kernel(h_ref, ss_ref, hp_ref, ssp_ref, wg_ref, wu_ref, cw_ref, cb_ref, wd_ref,
                 o_ref, wdc_ref, *, tiles_per_seq):
    wdc_ref[...] = wd_ref[...].astype(wdc_ref.dtype)
    i = pl.program_id(0)
    tm, d = h_ref.shape
    h = h_ref[...]
    rs = _row_scale(ss_ref, d)
    wg = wg_ref[...].astype(MXU_DTYPE)
    gate = _dot(h, wg) * rs
    up = _dot(h, wu_ref[...].astype(MXU_DTYPE)) * rs
    prev = _dot(hp_ref[...], wg) * _row_scale(ssp_ref, d)
    prev = jnp.where(i % tiles_per_seq == 0, 0.0, prev)
    row = lax.broadcasted_iota(jnp.int32, (tm, 1), 0)
    g1 = jnp.where(row == 0, prev[7:8], pltpu.roll(gate, 1, 0))
    g2 = pltpu.roll(gate, 2, 0)
    g2 = jnp.where(row == 0, prev[6:7], jnp.where(row == 1, prev[7:8], g2))
    cw = cw_ref[...]
    gc = cw[0:1] * g2 + cw[1:2] * g1 + cw[2:3] * gate + cb_ref[...]
    act = gc * (1.0 / (1.0 + jnp.exp(-gc))) * up
    o_ref[...] = act.astype(o_ref.dtype)


def _ffn1(h, ss, w_gu_all, conv_w_all, conv_b_all, w_down_all, layer, seq):
    m, d = h.shape
    dff = conv_w_all.shape[2]
    d_out = w_down_all.shape[2]
    tm = _pick(seq, 1024)
    tn = _pick(dff, 256)
    nj = dff // tn
    sub = 8
    n_steps = (m // tm) * nj
    slab = dff // n_steps
    assert slab * n_steps == dff and slab % 16 == 0
    return pl.pallas_call(
        functools.partial(_ffn1_kernel, tiles_per_seq=seq // tm),
        out_shape=(jax.ShapeDtypeStruct((m, dff), MXU_DTYPE),
                   jax.ShapeDtypeStruct((dff, d_out), MXU_DTYPE)),
        grid=(m // tm, nj),
        in_specs=[pl.BlockSpec((tm, d), lambda i, j: (i, 0)),
                  pl.BlockSpec((tm, LANES), lambda i, j: (i, 0)),
                  pl.BlockSpec((sub, d), lambda i, j: (jnp.maximum(i * (tm // sub) - 1, 0), 0)),
                  pl.BlockSpec((sub, LANES), lambda i, j: (jnp.maximum(i * (tm // sub) - 1, 0), 0)),
                  pl.BlockSpec((None, d, tn), lambda i, j: (layer, 0, j)),
                  pl.BlockSpec((None, d, tn), lambda i, j: (layer, 0, j + nj)),
                  pl.BlockSpec((None, CONV_WIDTH, tn), lambda i, j: (layer, 0, j)),
                  pl.BlockSpec((None, 1, tn), lambda i, j: (layer, 0, j)),
                  pl.BlockSpec((None, slab, d_out), lambda i, j: (layer, i * nj + j, 0))],
        out_specs=(pl.BlockSpec((tm, tn), lambda i, j: (i, j)),
                   pl.BlockSpec((slab, d_out), lambda i, j: (i * nj + j, 0))),
        compiler_params=_cparams("arbitrary", "arbitrary"),
        name="ffn_gate_up_glu",
    )(h, ss, h, ss, w_gu_all, w_gu_all, conv_w_all, conv_b_all.reshape(-1, 1, dff), w_down_all)


def kernel(x, attn_norm, w_in, a_q_norm, a_k_norm, lambda_qk, a_out_norm, b_q_norm, b_k_norm,
           rel_bias, w_out, ffn_norm, w_gate_up, conv_w, conv_b, w_down):
    bsz, s_len, d_model = x.shape
    depth = w_in.shape[0]
    m = bsz * s_len
    topk = min(TOPK_MAX, s_len // 4)
    assert s_len % (2 * DIFF_T) == 0 and s_len % DSA_FAR_W == 0 and s_len % IDX_KB == 0
    assert w_in.shape[2] == D_IN and w_out.shape[1] == A_V + B_Q

    near_bias_a = _near_bias(rel_bias[:, :A_HEADS], DIFF_T)
    near_bias_b = _near_bias(rel_bias[:, A_HEADS:], ATT_T)
    x2 = x.reshape(m, d_model)
    w_in_t = jnp.swapaxes(w_in, 1, 2)
    xg, ss = _norm_inputs(x2, attn_norm[0])
    for l in range(depth):
        lam_init = 0.8 - 0.6 * math.exp(-0.3 * l)
        proj = _matmul_nt(xg, ss, w_in_t, l, tm_pref=1024, tn_pref=512, name="in_proj")
        proj3 = proj.reshape(bsz, s_len, D_IN)
        two = lambda g: jnp.concatenate([g, g]).reshape(1, LANES)
        qa2, ka, va, qb, kb, vb, qi2, ki2 = _prep(
            proj3, two(a_q_norm[l]), two(a_k_norm[l]),
            b_q_norm[l].reshape(1, LANES), b_k_norm[l].reshape(1, LANES))
        wi_t = jnp.swapaxes(proj3[:, :, OFF_IW:OFF_IW + I_W], 1, 2)
        mix_a = _diff_attention(qa2, ka, va, near_bias_a, lambda_qk[l], a_out_norm[l], lam_init)
        mix_b = _dsa_attention(qi2, ki2, wi_t, qb, kb, vb, near_bias_b, topk)
        x2, xg, ss = _matmul2_res(mix_a, mix_b, w_out, l, x2, ffn_norm[l],
                                  tm_pref=1024, tn_pref=512, name="out_proj")
        act, w_down_c = _ffn1(xg, ss, w_gate_up, conv_w, conv_b, w_down, l, s_len)
        if l + 1 < depth:
            x2, xg, ss = _matmul_res(act, w_down_c, x2, attn_norm[l + 1],
                                     tm_pref=512, tn_pref=256, name="down_proj")
        else:
            x2 = _matmul_res(act, w_down_c, x2, tm_pref=512, tn_pref=256, name="down_proj")
    return x2.reshape(bsz, s_len, d_model)
```

```python
import functools
import math

import numpy as np
import jax
import jax.numpy as jnp
from jax import lax
from jax.experimental import pallas as pl
from jax.experimental.pallas import tpu as pltpu

CHUNK = 64
CHUNK_SHIFT = 6
A_HEADS = 16
A_QK_DIM = 64
A_V_DIM = 128
B_HEADS = 16
B_HEAD_DIM = 128
IDX_HEADS = 16
IDX_DIM = 64
TOPK_MAX = 256
N_BUCKETS = 32
MAX_DISTANCE = 128
CONV_WIDTH = 3
EPS = 1e-6

A_Q = A_HEADS * 2 * A_QK_DIM
A_K = A_HEADS * 2 * A_QK_DIM
A_V = A_HEADS * A_V_DIM
B_Q = B_HEADS * B_HEAD_DIM
B_K = B_HEAD_DIM
B_V = B_HEAD_DIM
I_Q = IDX_HEADS * IDX_DIM
I_K = IDX_DIM
I_W = IDX_HEADS
OFF_AQ = 0
OFF_AK = OFF_AQ + A_Q
OFF_AV = OFF_AK + A_K
OFF_BQ = OFF_AV + A_V
OFF_BK = OFF_BQ + B_Q
OFF_BV = OFF_BK + B_K
OFF_IQ = OFF_BV + B_V
OFF_IK = OFF_IQ + I_Q
OFF_IW = OFF_IK + I_K
D_IN = OFF_IW + I_W

LANES = 128
MXU_N = 256
V7X_VMEM_LIMIT = 56 * 1024 * 1024

DIFF_T = 256
DIFF_HEADS_PER_STEP = 8
ATT_T = 256
IDX_KB = 256
DSA_FAR_W = 512
DSA_HEADS_PER_GROUP = 2
MXU_DTYPE = jnp.bfloat16

LOG2E = math.log2(math.e)
NEG_INF = float("-inf")
M_FLOOR = -1e30
I16_MIN = -(2 ** 15)


def _cparams(*sem):
    return pltpu.CompilerParams(dimension_semantics=sem, vmem_limit_bytes=V7X_VMEM_LIMIT)


def _pick(n, pref):
    t = min(pref, n)
    while n % t:
        t -= LANES
    return t


def _dot(a, b):
    return jnp.dot(a, b, preferred_element_type=jnp.float32)


def _dot_nt(a, b):
    return lax.dot_general(a, b, (((1,), (1,)), ((), ())), preferred_element_type=jnp.float32)


def _row_sumsq(x):
    sq = functools.reduce(jnp.add, [x[:, c * LANES:(c + 1) * LANES] ** 2
                                    for c in range(x.shape[1] // LANES)])
    return jnp.broadcast_to(jnp.sum(sq, axis=-1, keepdims=True), (x.shape[0], LANES))


def _row_scale(ss_ref, d):
    return lax.rsqrt(ss_ref[:, 0:1] * (1.0 / d) + EPS)


def _col_chains(width):
    w = MXU_N if width % MXU_N == 0 else width
    return [slice(c, c + w) for c in range(0, width, w)]


def _accumulate_ss(ss_ref, j, parts):
    @pl.when(j == 0)
    def _():
        ss_ref[...] = jnp.zeros(ss_ref.shape, jnp.float32)

    ss_ref[...] += functools.reduce(jnp.add, parts)


def _norm_inputs_kernel(x_ref, g_ref, xg_ref, ss_ref):
    x = x_ref[...]
    xg_ref[...] = (x * g_ref[...]).astype(xg_ref.dtype)
    ss_ref[...] = _row_sumsq(x)


def _norm_inputs(x2d, g):
    m, d = x2d.shape
    tm = _pick(m, 256)
    return pl.pallas_call(
        _norm_inputs_kernel,
        out_shape=(jax.ShapeDtypeStruct((m, d), MXU_DTYPE),
                   jax.ShapeDtypeStruct((m, LANES), jnp.float32)),
        grid=(m // tm,),
        in_specs=[pl.BlockSpec((tm, d), lambda i: (i, 0)),
                  pl.BlockSpec((1, d), lambda i: (0, 0))],
        out_specs=(pl.BlockSpec((tm, d), lambda i: (i, 0)),
                   pl.BlockSpec((tm, LANES), lambda i: (i, 0))),
        compiler_params=_cparams("parallel"),
        name="norm_inputs",
    )(x2d, g.reshape(1, d))


def _mm_nt_kernel(a_ref, ss_ref, wt_ref, o_ref):
    a = a_ref[...]
    rs = _row_scale(ss_ref, a.shape[1])
    for cols in _col_chains(o_ref.shape[1]):
        acc = _dot_nt(a, wt_ref[cols, :].astype(MXU_DTYPE))
        o_ref[:, cols] = (acc * rs).astype(o_ref.dtype)


def _mm_res_kernel(a_ref, w_ref, r_ref, *rest, emit_norm):
    if emit_norm:
        g_ref, o_ref, xg_ref, ss_ref = rest
    else:
        (o_ref,) = rest
    a = a_ref[...]
    parts = []
    for cols in _col_chains(o_ref.shape[1]):
        x = r_ref[:, cols] + _dot(a, w_ref[:, cols])
        o_ref[:, cols] = x
        if emit_norm:
            xg_ref[:, cols] = (x * g_ref[:, cols]).astype(xg_ref.dtype)
            parts.append(_row_sumsq(x))
    if emit_norm:
        _accumulate_ss(ss_ref, pl.program_id(1), parts)


def _mm2_res_kernel(a1_ref, a2_ref, w1_ref, w2_ref, r_ref, g_ref, o_ref, xg_ref, ss_ref):
    a1, a2 = a1_ref[...], a2_ref[...]
    parts = []
    for cols in _col_chains(o_ref.shape[1]):
        acc = (_dot(a1, w1_ref[:, cols].astype(MXU_DTYPE))
               + _dot(a2, w2_ref[:, cols].astype(MXU_DTYPE)))
        x = r_ref[:, cols] + acc
        o_ref[:, cols] = x
        xg_ref[:, cols] = (x * g_ref[:, cols]).astype(xg_ref.dtype)
        parts.append(_row_sumsq(x))
    _accumulate_ss(ss_ref, pl.program_id(1), parts)


def _norm_out(m, n, tm, tn):
    shapes = (jax.ShapeDtypeStruct((m, n), jnp.float32),
              jax.ShapeDtypeStruct((m, n), MXU_DTYPE),
              jax.ShapeDtypeStruct((m, LANES), jnp.float32))
    specs = (pl.BlockSpec((tm, tn), lambda i, j: (i, j)),
             pl.BlockSpec((tm, tn), lambda i, j: (i, j)),
             pl.BlockSpec((tm, LANES), lambda i, j: (i, 0)))
    return shapes, specs


def _matmul2_res(a1, a2, w_all, layer, res, g_next, *, tm_pref, tn_pref, name):
    m, k = a1.shape
    n = w_all.shape[2]
    tm, tn = _pick(m, tm_pref), _pick(n, tn_pref)
    shapes, specs = _norm_out(m, n, tm, tn)
    return pl.pallas_call(
        _mm2_res_kernel,
        out_shape=shapes,
        grid=(m // tm, n // tn),
        in_specs=[pl.BlockSpec((tm, k), lambda i, j: (i, 0)),
                  pl.BlockSpec((tm, k), lambda i, j: (i, 0)),
                  pl.BlockSpec((None, k, tn), lambda i, j: (layer, 0, j)),
                  pl.BlockSpec((None, k, tn), lambda i, j: (layer, 1, j)),
                  pl.BlockSpec((tm, tn), lambda i, j: (i, j)),
                  pl.BlockSpec((1, tn), lambda i, j: (0, j))],
        out_specs=specs,
        compiler_params=_cparams("parallel", "arbitrary"),
        name=name,
    )(a1, a2, w_all, w_all, res, g_next.reshape(1, n))


def _matmul_nt(a, ss, wt_all, layer, *, tm_pref, tn_pref, name):
    m, k = a.shape
    n = wt_all.shape[1]
    tm, tn = _pick(m, tm_pref), min(tn_pref, n)
    return pl.pallas_call(
        _mm_nt_kernel,
        out_shape=jax.ShapeDtypeStruct((m, n), jnp.float32),
        grid=(m // tm, pl.cdiv(n, tn)),
        in_specs=[pl.BlockSpec((tm, k), lambda i, j: (i, 0)),
                  pl.BlockSpec((tm, LANES), lambda i, j: (i, 0)),
                  pl.BlockSpec((None, tn, k), lambda i, j: (layer, j, 0))],
        out_specs=pl.BlockSpec((tm, tn), lambda i, j: (i, j)),
        compiler_params=_cparams("parallel", "arbitrary"),
        name=name,
    )(a, ss, wt_all)


def _matmul_res(a, w, res, g_next=None, *, tm_pref, tn_pref, name):
    m, k = a.shape
    n = w.shape[1]
    tm, tn = _pick(m, tm_pref), _pick(n, tn_pref)
    in_specs = [pl.BlockSpec((tm, k), lambda i, j: (i, 0)),
                pl.BlockSpec((k, tn), lambda i, j: (0, j)),
                pl.BlockSpec((tm, tn), lambda i, j: (i, j))]
    args = [a, w, res]
    if g_next is None:
        shapes = jax.ShapeDtypeStruct((m, n), jnp.float32)
        specs = pl.BlockSpec((tm, tn), lambda i, j: (i, j))
    else:
        shapes, specs = _norm_out(m, n, tm, tn)
        in_specs.append(pl.BlockSpec((1, tn), lambda i, j: (0, j)))
        args.append(g_next.reshape(1, n))
    return pl.pallas_call(
        functools.partial(_mm_res_kernel, emit_norm=g_next is not None),
        out_shape=shapes,
        grid=(m // tm, n // tn),
        in_specs=in_specs,
        out_specs=specs,
        compiler_params=_cparams("parallel", "arbitrary"),
        name=name,
    )(*args)


def _seg_ones(seg):
    r = lax.broadcasted_iota(jnp.int32, (LANES, LANES), 0) // seg
    c = lax.broadcasted_iota(jnp.int32, (LANES, LANES), 1) // seg
    return (r == c).astype(MXU_DTYPE)


def _seg_rms(x, ones, seg):
    sq = x * x
    hi = sq.astype(MXU_DTYPE)
    lo = (sq - hi.astype(jnp.float32)).astype(MXU_DTYPE)
    ss = _dot(hi, ones) + _dot(lo, ones)
    return x * lax.rsqrt(ss * (1.0 / seg) + EPS)


def _prep_kernel(p_ref, gaq_ref, gak_ref, gbq_ref, gbk_ref,
                 qa2_ref, ka_ref, va_ref, qb_ref, kb_ref, vb_ref, qi2_ref, ki2_ref):
    ones64 = _seg_ones(A_QK_DIM)
    ones128 = _seg_ones(B_HEAD_DIM)
    lane = lax.broadcasted_iota(jnp.int32, (1, LANES), 1)
    lo_half = lane < A_QK_DIM
    gaq, gak, gbq, gbk = gaq_ref[...], gak_ref[...], gbq_ref[...], gbk_ref[...]
    a_scale = A_QK_DIM ** -0.5 * LOG2E
    b_scale = B_HEAD_DIM ** -0.5 * LOG2E
    i_scale = IDX_DIM ** -0.5

    for h in range(A_HEADS):
        q = _seg_rms(p_ref[0, :, OFF_AQ + h * LANES:OFF_AQ + (h + 1) * LANES], ones64, A_QK_DIM)
        q = q * gaq * a_scale
        qa2_ref[0, h, 0] = jnp.where(lo_half, q, 0.0).astype(qa2_ref.dtype)
        qa2_ref[0, h, 1] = jnp.where(lo_half, 0.0, q).astype(qa2_ref.dtype)
        k = _seg_rms(p_ref[0, :, OFF_AK + h * LANES:OFF_AK + (h + 1) * LANES], ones64, A_QK_DIM)
        ka_ref[0, h] = (k * gak).astype(ka_ref.dtype)
        va_ref[0, h] = p_ref[0, :, OFF_AV + h * LANES:OFF_AV + (h + 1) * LANES].astype(va_ref.dtype)
    for h in range(B_HEADS):
        q = _seg_rms(p_ref[0, :, OFF_BQ + h * LANES:OFF_BQ + (h + 1) * LANES], ones128, B_HEAD_DIM)
        qb_ref[0, h] = (q * gbq * b_scale).astype(qb_ref.dtype)
    k = _seg_rms(p_ref[0, :, OFF_BK:OFF_BK + B_K], ones128, B_HEAD_DIM)
    kb_ref[0] = (k * gbk).astype(kb_ref.dtype)
    vb_ref[0] = p_ref[0, :, OFF_BV:OFF_BV + B_V].astype(vb_ref.dtype)
    for hp in range(IDX_HEADS // 2):
        qi = p_ref[0, :, OFF_IQ + hp * LANES:OFF_IQ + (hp + 1) * LANES] * i_scale
        qi2_ref[0, 2 * hp] = jnp.where(lo_half, qi, 0.0).astype(qi2_ref.dtype)
        qi2_ref[0, 2 * hp + 1] = jnp.where(lo_half, 0.0, qi).astype(qi2_ref.dtype)
    kt = p_ref[0, :, OFF_IK:OFF_IK + I_K]
    ki2_ref[0] = jnp.concatenate([kt, kt], axis=-1).astype(ki2_ref.dtype)


def _prep(proj3, gaq, gak, gbq, gbk):
    b, s, npad = proj3.shape
    ts = _pick(s, 256)
    dt = MXU_DTYPE
    head4 = lambda nh: jax.ShapeDtypeStruct((b, nh, s, LANES), dt)
    flat3 = jax.ShapeDtypeStruct((b, s, LANES), dt)
    hspec = lambda nh: pl.BlockSpec((1, nh, ts, LANES), lambda bi, i: (bi, 0, i, 0))
    fspec = pl.BlockSpec((1, ts, LANES), lambda bi, i: (bi, i, 0))
    gspec = pl.BlockSpec((1, LANES), lambda bi, i: (0, 0))
    return pl.pallas_call(
        _prep_kernel,
        out_shape=(jax.ShapeDtypeStruct((b, A_HEADS, 2, s, LANES), dt),
                   head4(A_HEADS), head4(A_HEADS), head4(B_HEADS), flat3, flat3,
                   head4(IDX_HEADS), flat3),
        grid=(b, s // ts),
        in_specs=[pl.BlockSpec((1, ts, npad), lambda bi, i: (bi, i, 0)), gspec, gspec, gspec, gspec],
        out_specs=(pl.BlockSpec((1, A_HEADS, 2, ts, LANES), lambda bi, i: (bi, 0, 0, i, 0)),
                   hspec(A_HEADS), hspec(A_HEADS), hspec(B_HEADS), fspec, fspec,
                   hspec(IDX_HEADS), fspec),
        compiler_params=_cparams("parallel", "parallel"),
        name="head_prep",
    )(proj3, gaq, gak, gbq, gbk)


def _rel_bucket_np(rel):
    nb = N_BUCKETS // 2
    max_exact = nb // 2
    bucket = np.where(rel > 0, nb, 0)
    n = np.abs(rel)
    nf = np.maximum(n, 1).astype(np.float32)
    large = max_exact + (np.log(nf / np.float32(max_exact)) / np.float32(math.log(MAX_DISTANCE / max_exact))
                         * np.float32(nb - max_exact)).astype(np.int32)
    large = np.minimum(large, nb - 1)
    return (bucket + np.where(n < max_exact, n, large)).astype(np.int32)


def _near_bucket_ids(t):
    tq = np.arange(t)[:, None]
    out = []
    for key0 in (0, -t):
        ts = key0 + np.arange(2 * t)[None, :]
        ids = _rel_bucket_np(ts - tq)
        vis = np.floor_divide(ts, CHUNK) <= np.floor_divide(tq, CHUNK)
        out.append(np.where(vis, ids, -1))
    return np.stack(out).astype(np.int32)


FAR_BUCKET = N_BUCKETS // 2 - 1


def _bias_kernel(tab_ref, ids_ref, o_ref):
    h = pl.program_id(0)
    ids = ids_ref[...]
    far = tab_ref[FAR_BUCKET, h]
    acc = jnp.full(ids.shape, NEG_INF, jnp.float32)
    for bkt in range(N_BUCKETS):
        acc = jnp.where(ids == bkt, (tab_ref[bkt, h] - far) * LOG2E, acc)
    o_ref[0] = acc


def _near_bias(rel_bias, t):
    ids = jnp.asarray(_near_bucket_ids(t))
    nh = rel_bias.shape[1]
    return pl.pallas_call(
        _bias_kernel,
        out_shape=jax.ShapeDtypeStruct((nh, 2, t, 2 * t), jnp.float32),
        grid=(nh,),
        in_specs=[pl.BlockSpec(memory_space=pltpu.SMEM),
                  pl.BlockSpec((2, t, 2 * t), lambda h: (0, 0, 0))],
        out_specs=pl.BlockSpec((1, 2, t, 2 * t), lambda h: (h, 0, 0, 0)),
        compiler_params=_cparams("arbitrary"),
        name="near_bias",
    )(rel_bias, ids)


def _softmax_step(s, v, m_sc, l_sc, acc_sc, idx, first):
    rows, width = s.shape
    blocks = [s[:, c * LANES:(c + 1) * LANES] for c in range(width // LANES)]
    m_part = functools.reduce(jnp.maximum, blocks)
    m_cur = jnp.broadcast_to(jnp.max(m_part, axis=-1, keepdims=True), (rows, LANES))
    if first:
        m_new = jnp.maximum(m_cur, M_FLOOR)
    else:
        m_prev = m_sc[idx]
        m_new = jnp.maximum(m_prev, m_cur)
        alpha = jnp.exp2(m_prev - m_new)
    ps = [jnp.exp2(blk - m_new) for blk in blocks]
    l_part = functools.reduce(jnp.add, ps)
    l_cur = jnp.broadcast_to(jnp.sum(l_part, axis=-1, keepdims=True), (rows, LANES))
    p = jnp.concatenate([x.astype(v.dtype) for x in ps], axis=-1)
    pv = _dot(p, v)
    if first:
        l_sc[idx] = l_cur
        acc_sc[idx] = pv
    else:
        l_sc[idx] = alpha * l_sc[idx] + l_cur
        acc_sc[idx] = alpha * acc_sc[idx] + pv
    m_sc[idx] = m_new


def _diff_kernel(q2_ref, k_ref, v_ref, bias_ref, lam_ref, g_ref, o_ref, m_sc, l_sc, acc_sc,
                 *, lam_init, t, hg):
    i = pl.program_id(2)
    n_far = jnp.maximum(i - 1, 0)
    near0 = pl.multiple_of(n_far * t, t)

    def step(k0, width, first):
        for hh in range(hg):
            q2 = q2_ref[0, hh].reshape(2 * t, LANES)
            s = _dot_nt(q2, k_ref[0, hh, pl.ds(k0, width), :])
            if first:
                s = (s.reshape(2, t, width) + bias_ref[hh, 0][None]).reshape(2 * t, width)
            _softmax_step(s, v_ref[0, hh, pl.ds(k0, width), :], m_sc, l_sc, acc_sc, hh, first)

    step(near0, 2 * t, True)

    def far_body(j, carry):
        step(pl.multiple_of(j * 2 * t, 2 * t), 2 * t, False)
        return carry

    lax.fori_loop(0, n_far // 2, far_body, 0)

    @pl.when(n_far % 2 == 1)
    def _():
        step(pl.multiple_of((n_far - 1) * t, t), t, False)

    lq = lam_ref[...]
    lam = (jnp.exp(jnp.sum(lq[0:1] * lq[1:2], axis=-1, keepdims=True))
           - jnp.exp(jnp.sum(lq[2:3] * lq[3:4], axis=-1, keepdims=True)) + lam_init)
    for hh in range(hg):
        o = acc_sc[hh] / l_sc[hh]
        o = o[:t] - lam * o[t:]
        ms = jnp.mean(o * o, axis=-1, keepdims=True)
        o = o * lax.rsqrt(ms + EPS) * g_ref[...]
        o_ref[:, hh * A_V_DIM:(hh + 1) * A_V_DIM] = (o * (1.0 - lam_init)).astype(o_ref.dtype)


def _diff_attention(qa2, ka, va, bias, lam_qk, g_out, lam_init):
    b, nh, _, s, _ = qa2.shape
    t = DIFF_T
    hg = DIFF_HEADS_PER_STEP
    nt = s // t
    rows = 2 * t
    return pl.pallas_call(
        functools.partial(_diff_kernel, lam_init=lam_init, t=t, hg=hg),
        out_shape=jax.ShapeDtypeStruct((b * s, nh * A_V_DIM), MXU_DTYPE),
        grid=(b, nh // hg, nt),
        in_specs=[pl.BlockSpec((1, hg, 2, t, LANES), lambda bi, h, i: (bi, h, 0, i, 0)),
                  pl.BlockSpec((1, hg, s, LANES), lambda bi, h, i: (bi, h, 0, 0)),
                  pl.BlockSpec((1, hg, s, LANES), lambda bi, h, i: (bi, h, 0, 0)),
                  pl.BlockSpec((hg, 1, t, 2 * t), lambda bi, h, i: (h, jnp.minimum(i, 1), 0, 0)),
                  pl.BlockSpec((4, A_QK_DIM), lambda bi, h, i: (0, 0)),
                  pl.BlockSpec((1, A_V_DIM), lambda bi, h, i: (0, 0))],
        out_specs=pl.BlockSpec((t, hg * A_V_DIM), lambda bi, h, i: (bi * nt + i, h)),
        scratch_shapes=[pltpu.VMEM((hg, rows, LANES), jnp.float32),
                        pltpu.VMEM((hg, rows, LANES), jnp.float32),
                        pltpu.VMEM((hg, rows, A_V_DIM), jnp.float32)],
        compiler_params=_cparams("parallel", "parallel", "arbitrary"),
        name="diff_attention",
    )(qa2, ka, va, bias, lam_qk, g_out.reshape(1, A_V_DIM))


def _sortable_key(x):
    bits = pltpu.bitcast(x, jnp.int32)
    return bits ^ ((bits >> 31) & 0x7FFFFFFF)


def _dsa_kernel(qi2_ref, ki2_ref, wi_ref, qb_ref, kb_ref, vb_ref, bias_ref, o_ref,
                key_sc, hi_sc, lo_sc, am_sc, m_sc, l_sc, acc_sc, *, topk, heads_per_group):
    t = ATT_T
    kb_sz = IDX_KB
    i = pl.program_id(1)
    q0 = i * t
    n_idx_blocks = (jnp.maximum(q0 + t, 2 * t) + kb_sz - 1) // kb_sz
    t_idx = q0 + lax.broadcasted_iota(jnp.int32, (1, t), 1)
    t_chunk = t_idx >> CHUNK_SHIFT

    wi = wi_ref[0] * (IDX_HEADS ** -0.5)

    def score_body(kbi, carry):
        k0 = pl.multiple_of(kbi * kb_sz, kb_sz)
        kblk = ki2_ref[0, pl.ds(k0, kb_sz), :]
        sc = jnp.zeros((kb_sz, t), jnp.float32)
        for h in range(IDX_HEADS):
            d = _dot_nt(kblk, qi2_ref[0, h])
            sc = sc + jnp.maximum(d, 0.0) * wi[h:h + 1, :]
        s_chunk = (k0 + lax.broadcasted_iota(jnp.int32, (kb_sz, 1), 0)) >> CHUNK_SHIFT
        sc = jnp.where(s_chunk <= t_chunk, sc, NEG_INF)
        key = _sortable_key(sc)
        key_sc[pl.ds(k0, kb_sz), :] = key
        hi_sc[pl.ds(k0, kb_sz), :] = (key >> 16).astype(jnp.int16)
        lo_sc[pl.ds(k0, kb_sz), :] = ((key & 0xFFFF) + I16_MIN).astype(jnp.int16)
        return carry

    lax.fori_loop(0, n_idx_blocks, score_body, 0)

    kk = jnp.minimum((t_chunk + 1) * CHUNK, topk)

    def count16(ref, cand, strict):
        cand16 = cand.astype(jnp.int16)

        def body(kbi, c):
            k0 = pl.multiple_of(kbi * kb_sz, kb_sz)
            blk = ref[pl.ds(k0, kb_sz), :]
            hit = (blk > cand16) if strict else (blk >= cand16)
            h3 = hit.astype(jnp.int16).reshape(kb_sz // 16, 16, t)
            for r in range(kb_sz // 16):
                c = c + h3[r]
            return c

        c16 = lax.fori_loop(0, n_idx_blocks, body, jnp.zeros((16, t), jnp.int16))
        return jnp.sum(c16.astype(jnp.int32), axis=0, keepdims=True)

    def search16(ref, base):
        def bit_body(it, thr):
            cand = thr + jnp.left_shift(jnp.int32(1), 15 - it)
            return jnp.where(base + count16(ref, cand, False) >= kk, cand, thr)
        return lax.fori_loop(0, 16, bit_body, jnp.full((1, t), I16_MIN, jnp.int32))

    thr_hi = search16(hi_sc, jnp.zeros((1, t), jnp.int32))
    above = count16(hi_sc, thr_hi, True)
    thr_hi16 = thr_hi.astype(jnp.int16)

    def tie_body(kbi, carry):
        k0 = pl.multiple_of(kbi * kb_sz, kb_sz)
        same = hi_sc[pl.ds(k0, kb_sz), :] == thr_hi16
        lo_sc[pl.ds(k0, kb_sz), :] = jnp.where(same, lo_sc[pl.ds(k0, kb_sz), :], jnp.int16(I16_MIN))
        return carry

    lax.fori_loop(0, n_idx_blocks, tie_body, 0)
    thr_lo = search16(lo_sc, above)
    thr = (thr_hi << 16) + (thr_lo - I16_MIN)
    n_greater = above + count16(lo_sc, thr_lo, True)
    slots = (kk - n_greater).astype(jnp.float32)

    tri = (lax.broadcasted_iota(jnp.int32, (kb_sz, kb_sz), 0)
           >= lax.broadcasted_iota(jnp.int32, (kb_sz, kb_sz), 1)).astype(MXU_DTYPE)

    def mask_body(kbi, ties_before):
        k0 = pl.multiple_of(kbi * kb_sz, kb_sz)
        key = key_sc[pl.ds(k0, kb_sz), :]
        tie = key == thr
        rank = ties_before + _dot(tri, tie.astype(MXU_DTYPE))
        sel = (key > thr) | (tie & (rank <= slots))
        am_sc[:, pl.ds(k0, kb_sz)] = jnp.where(sel, 0.0, NEG_INF).T
        return rank[kb_sz - 1:kb_sz, :]

    lax.fori_loop(0, n_idx_blocks, mask_body, jnp.zeros((1, t), jnp.float32))

    g = heads_per_group
    n_groups = B_HEADS // g
    far_w = DSA_FAR_W
    near0 = pl.multiple_of(jnp.maximum(i - 1, 0) * t, t)
    am_near = am_sc[:, pl.ds(near0, 2 * t)]
    am_sc[:, pl.ds(near0, far_w + 2 * t)] = jnp.full((t, far_w + 2 * t), NEG_INF, jnp.float32)

    def step(k0, width, first):
        kblk = kb_ref[0, pl.ds(k0, width), :]
        vblk = vb_ref[0, pl.ds(k0, width), :]
        for hg in range(n_groups):
            q = qb_ref[0, hg * g:(hg + 1) * g].reshape(g * t, LANES)
            s = _dot_nt(q, kblk).reshape(g, t, width)
            if first:
                s = s + (bias_ref[hg * g:(hg + 1) * g, 0] + am_near[None])
            else:
                s = s + am_sc[:, pl.ds(k0, width)][None]
            _softmax_step(s.reshape(g * t, width), vblk, m_sc, l_sc, acc_sc, hg, first)

    step(near0, 2 * t, True)

    def far_body(j, carry):
        step(pl.multiple_of(j * far_w, far_w), far_w, False)
        return carry

    lax.fori_loop(0, (near0 + far_w - 1) // far_w, far_body, 0)

    for hg in range(n_groups):
        o = acc_sc[hg] / l_sc[hg]
        for hh in range(g):
            h = hg * g + hh
            o_ref[:, h * B_HEAD_DIM:(h + 1) * B_HEAD_DIM] = o[hh * t:(hh + 1) * t].astype(o_ref.dtype)


def _dsa_attention(qi2, ki2, wi_t, qb, kb, vb, bias, topk):
    b, nh, s, _ = qb.shape
    t = ATT_T
    nt = s // t
    g = DSA_HEADS_PER_GROUP
    n_groups = nh // g
    nb_cols = B_HEADS * B_HEAD_DIM
    return pl.pallas_call(
        functools.partial(_dsa_kernel, topk=topk, heads_per_group=g),
        out_shape=jax.ShapeDtypeStruct((b * s, nb_cols), MXU_DTYPE),
        grid=(b, nt),
        in_specs=[pl.BlockSpec((1, IDX_HEADS, t, LANES), lambda bi, i: (bi, 0, i, 0)),
                  pl.BlockSpec((1, s, LANES), lambda bi, i: (bi, 0, 0)),
                  pl.BlockSpec((1, IDX_HEADS, t), lambda bi, i: (bi, 0, i)),
                  pl.BlockSpec((1, nh, t, LANES), lambda bi, i: (bi, 0, i, 0)),
                  pl.BlockSpec((1, s, LANES), lambda bi, i: (bi, 0, 0)),
                  pl.BlockSpec((1, s, LANES), lambda bi, i: (bi, 0, 0)),
                  pl.BlockSpec((B_HEADS, 1, t, 2 * t), lambda bi, i: (0, jnp.minimum(i, 1), 0, 0))],
        out_specs=pl.BlockSpec((t, nb_cols), lambda bi, i: (bi * nt + i, 0)),
        scratch_shapes=[pltpu.VMEM((s, t), jnp.int32),
                        pltpu.VMEM((s, t), jnp.int16),
                        pltpu.VMEM((s, t), jnp.int16),
                        pltpu.VMEM((t, s + DSA_FAR_W), jnp.float32),
                        pltpu.VMEM((n_groups, g * t, LANES), jnp.float32),
                        pltpu.VMEM((n_groups, g * t, LANES), jnp.float32),
                        pltpu.VMEM((n_groups, g * t, B_HEAD_DIM), jnp.float32)],
        compiler_params=_cparams("parallel", "arbitrary"),
        name="dsa_attention",
    )(qi2, ki2, wi_t, qb, kb, vb, bias)


def _ffn1_kernel(h_ref, ss_ref, hp_ref, ssp_ref, wg_ref, wu_ref, cw_ref, cb_ref, wd_ref,
                 o_ref, wdc_ref, *, tiles_per_seq):
    wdc_ref[...] = wd_ref[...].astype(wdc_ref.dtype)
    i = pl.program_id(0)
    tm, d = h_ref.shape
    h = h_ref[...]
    rs = _row_scale(ss_ref, d)
    wg = wg_ref[...].astype(MXU_DTYPE)
    gate = _dot(h, wg) * rs
    up = _dot(h, wu_ref[...].astype(MXU_DTYPE)) * rs
    prev = _dot(hp_ref[...], wg) * _row_scale(ssp_ref, d)
    prev = jnp.where(i % tiles_per_seq == 0, 0.0, prev)
    cw = cw_ref[...]
    cb = cb_ref[...]

    def glu(g2, g1, g0, u):
        gc = cw[0:1] * g2 + cw[1:2] * g1 + cw[2:3] * g0 + cb
        return (gc * (1.0 / (1.0 + jnp.exp(-gc))) * u).astype(o_ref.dtype)

    o_ref[...] = glu(pltpu.roll(gate, 2, 0), pltpu.roll(gate, 1, 0), gate, up)
    top = gate[0:8]
    row = lax.broadcasted_iota(jnp.int32, (8, 1), 0)
    t1 = jnp.where(row == 0, prev[7:8], pltpu.roll(top, 1, 0))
    t2 = jnp.where(row == 0, prev[6:7], jnp.where(row == 1, prev[7:8], pltpu.roll(top, 2, 0)))
    o_ref[0:8, :] = glu(t2, t1, top, up[0:8])


def _ffn1(h, ss, w_gu_all, conv_w_all, conv_b_all, w_down_all, layer, seq):
    m, d = h.shape
    dff = conv_w_all.shape[2]
    d_out = w_down_all.shape[2]
    tm = _pick(seq, 1024)
    tn = _pick(dff, 256)
    nj = dff // tn
    sub = 8
    n_steps = (m // tm) * nj
    slab = dff // n_steps
    assert slab * n_steps == dff and slab % 16 == 0
    return pl.pallas_call(
        functools.partial(_ffn1_kernel, tiles_per_seq=seq // tm),
        out_shape=(jax.ShapeDtypeStruct((m, dff), MXU_DTYPE),
                   jax.ShapeDtypeStruct((dff, d_out), MXU_DTYPE)),
        grid=(m // tm, nj),
        in_specs=[pl.BlockSpec((tm, d), lambda i, j: (i, 0)),
                  pl.BlockSpec((tm, LANES), lambda i, j: (i, 0)),
                  pl.BlockSpec((sub, d), lambda i, j: (jnp.maximum(i * (tm // sub) - 1, 0), 0)),
                  pl.BlockSpec((sub, LANES), lambda i, j: (jnp.maximum(i * (tm // sub) - 1, 0), 0)),
                  pl.BlockSpec((None, d, tn), lambda i, j: (layer, 0, j)),
                  pl.BlockSpec((None, d, tn), lambda i, j: (layer, 0, j + nj)),
                  pl.BlockSpec((None, CONV_WIDTH, tn), lambda i, j: (layer, 0, j)),
                  pl.BlockSpec((None, 1, tn), lambda i, j: (layer, 0, j)),
                  pl.BlockSpec((None, slab, d_out), lambda i, j: (layer, i * nj + j, 0))],
        out_specs=(pl.BlockSpec((tm, tn), lambda i, j: (i, j)),
                   pl.BlockSpec((slab, d_out), lambda i, j: (i * nj + j, 0))),
        compiler_params=_cparams("arbitrary", "arbitrary"),
        name="ffn_gate_up_glu",
    )(h, ss, h, ss, w_gu_all, w_gu_all, conv_w_all, conv_b_all.reshape(-1, 1, dff), w_down_all)


def kernel(x, attn_norm, w_in, a_q_norm, a_k_norm, lambda_qk, a_out_norm, b_q_norm, b_k_norm,
           rel_bias, w_out, ffn_norm, w_gate_up, conv_w, conv_b, w_down):
    bsz, s_len, d_model = x.shape
    depth = w_in.shape[0]
    m = bsz * s_len
    topk = min(TOPK_MAX, s_len // 4)
    assert s_len % (2 * DIFF_T) == 0 and s_len % DSA_FAR_W == 0 and s_len % IDX_KB == 0
    assert w_in.shape[2] == D_IN and w_out.shape[1] == A_V + B_Q

    near_bias_a = _near_bias(rel_bias[:, :A_HEADS], DIFF_T)
    near_bias_b = _near_bias(rel_bias[:, A_HEADS:], ATT_T)
    x2 = x.reshape(m, d_model)
    w_in_t = jnp.swapaxes(w_in, 1, 2)
    xg, ss = _norm_inputs(x2, attn_norm[0])
    for l in range(depth):
        lam_init = 0.8 - 0.6 * math.exp(-0.3 * l)
        proj = _matmul_nt(xg, ss, w_in_t, l, tm_pref=1024, tn_pref=512, name="in_proj")
        proj3 = proj.reshape(bsz, s_len, D_IN)
        two = lambda g: jnp.concatenate([g, g]).reshape(1, LANES)
        qa2, ka, va, qb, kb, vb, qi2, ki2 = _prep(
            proj3, two(a_q_norm[l]), two(a_k_norm[l]),
            b_q_norm[l].reshape(1, LANES), b_k_norm[l].reshape(1, LANES))
        wi_t = jnp.swapaxes(proj3[:, :, OFF_IW:OFF_IW + I_W], 1, 2)
        mix_a = _diff_attention(qa2, ka, va, near_bias_a, lambda_qk[l], a_out_norm[l], lam_init)
        mix_b = _dsa_attention(qi2, ki2, wi_t, qb, kb, vb, near_bias_b, topk)
        x2, xg, ss = _matmul2_res(mix_a, mix_b, w_out, l, x2, ffn_norm[l],
                                  tm_pref=1024, tn_pref=512, name="out_proj")
        act, w_down_c = _ffn1(xg, ss, w_gate_up, conv_w, conv_b, w_down, l, s_len)
        if l + 1 < depth:
            x2, xg, ss = _matmul_res(act, w_down_c, x2, attn_norm[l + 1],
                                     tm_pref=512, tn_pref=512, name="down_proj")
        else:
            x2 = _matmul_res(act, w_down_c, x2, tm_pref=512, tn_pref=512, name="down_proj")
    return x2.reshape(bsz, s_len, d_model)
```

```python
import functools
import math

import numpy as np
import jax
import jax.numpy as jnp
from jax import lax
from jax.experimental import pallas as pl
from jax.experimental.pallas import tpu as pltpu

CHUNK = 64
CHUNK_SHIFT = 6
A_HEADS = 16
A_QK_DIM = 64
A_V_DIM = 128
B_HEADS = 16
B_HEAD_DIM = 128
IDX_HEADS = 16
IDX_DIM = 64
TOPK_MAX = 256
N_BUCKETS = 32
MAX_DISTANCE = 128
CONV_WIDTH = 3
EPS = 1e-6

A_Q = A_HEADS * 2 * A_QK_DIM
A_K = A_HEADS * 2 * A_QK_DIM
A_V = A_HEADS * A_V_DIM
B_Q = B_HEADS * B_HEAD_DIM
B_K = B_HEAD_DIM
B_V = B_HEAD_DIM
I_Q = IDX_HEADS * IDX_DIM
I_K = IDX_DIM
I_W = IDX_HEADS
OFF_AQ = 0
OFF_AK = OFF_AQ + A_Q
OFF_AV = OFF_AK + A_K
OFF_BQ = OFF_AV + A_V
OFF_BK = OFF_BQ + B_Q
OFF_BV = OFF_BK + B_K
OFF_IQ = OFF_BV + B_V
OFF_IK = OFF_IQ + I_Q
OFF_IW = OFF_IK + I_K
D_IN = OFF_IW + I_W

LANES = 128
MXU_N = 256
V7X_VMEM_LIMIT = 58 * 1024 * 1024

DIFF_T = 256
DIFF_HEADS_PER_STEP = 8
ATT_T = 256
IDX_KB = 256
DSA_FAR_W = 512
DSA_HEADS_PER_GROUP = 2
MXU_DTYPE = jnp.bfloat16

LOG2E = math.log2(math.e)
NEG_INF = float("-inf")
M_FLOOR = -1e30
I16_MIN = -(2 ** 15)


def _cparams(*sem):
    return pltpu.CompilerParams(dimension_semantics=sem, vmem_limit_bytes=V7X_VMEM_LIMIT)


def _pick(n, pref):
    t = min(pref, n)
    while n % t:
        t -= LANES
    return t


def _dot(a, b):
    return jnp.dot(a, b, preferred_element_type=jnp.float32)


def _dot_nt(a, b):
    return lax.dot_general(a, b, (((1,), (1,)), ((), ())), preferred_element_type=jnp.float32)


def _row_sumsq(x):
    sq = functools.reduce(jnp.add, [x[:, c * LANES:(c + 1) * LANES] ** 2
                                    for c in range(x.shape[1] // LANES)])
    return jnp.broadcast_to(jnp.sum(sq, axis=-1, keepdims=True), (x.shape[0], LANES))


def _row_scale(ss_ref, d):
    return lax.rsqrt(ss_ref[:, 0:1] * (1.0 / d) + EPS)


def _col_chains(width):
    w = MXU_N if width % MXU_N == 0 else width
    return [slice(c, c + w) for c in range(0, width, w)]


def _accumulate_ss(ss_ref, j, parts):
    @pl.when(j == 0)
    def _():
        ss_ref[...] = jnp.zeros(ss_ref.shape, jnp.float32)

    ss_ref[...] += functools.reduce(jnp.add, parts)


def _norm_inputs_kernel(x_ref, g_ref, xg_ref, ss_ref):
    x = x_ref[...]
    xg_ref[...] = (x * g_ref[...]).astype(xg_ref.dtype)
    ss_ref[...] = _row_sumsq(x)


def _norm_inputs(x2d, g):
    m, d = x2d.shape
    tm = _pick(m, 256)
    return pl.pallas_call(
        _norm_inputs_kernel,
        out_shape=(jax.ShapeDtypeStruct((m, d), MXU_DTYPE),
                   jax.ShapeDtypeStruct((m, LANES), jnp.float32)),
        grid=(m // tm,),
        in_specs=[pl.BlockSpec((tm, d), lambda i: (i, 0)),
                  pl.BlockSpec((1, d), lambda i: (0, 0))],
        out_specs=(pl.BlockSpec((tm, d), lambda i: (i, 0)),
                   pl.BlockSpec((tm, LANES), lambda i: (i, 0))),
        compiler_params=_cparams("parallel"),
        name="norm_inputs",
    )(x2d, g.reshape(1, d))


def _mm_nt_kernel(a_ref, ss_ref, wt_ref, o_ref):
    a = a_ref[...]
    rs = _row_scale(ss_ref, a.shape[1])
    for cols in _col_chains(o_ref.shape[1]):
        acc = _dot_nt(a, wt_ref[cols, :].astype(MXU_DTYPE))
        o_ref[:, cols] = (acc * rs).astype(o_ref.dtype)


def _mm_res_kernel(a_ref, w_ref, r_ref, *rest, emit_norm):
    if emit_norm:
        g_ref, o_ref, xg_ref, ss_ref = rest
    else:
        (o_ref,) = rest
    a = a_ref[...]
    parts = []
    for cols in _col_chains(o_ref.shape[1]):
        x = r_ref[:, cols] + _dot(a, w_ref[:, cols])
        o_ref[:, cols] = x
        if emit_norm:
            xg_ref[:, cols] = (x * g_ref[:, cols]).astype(xg_ref.dtype)
            parts.append(_row_sumsq(x))
    if emit_norm:
        _accumulate_ss(ss_ref, pl.program_id(1), parts)


def _mm2_res_kernel(a1_ref, a2_ref, w1_ref, w2_ref, r_ref, g_ref, o_ref, xg_ref, ss_ref):
    a1, a2 = a1_ref[...], a2_ref[...]
    parts = []
    for cols in _col_chains(o_ref.shape[1]):
        acc = (_dot(a1, w1_ref[:, cols].astype(MXU_DTYPE))
               + _dot(a2, w2_ref[:, cols].astype(MXU_DTYPE)))
        x = r_ref[:, cols] + acc
        o_ref[:, cols] = x
        xg_ref[:, cols] = (x * g_ref[:, cols]).astype(xg_ref.dtype)
        parts.append(_row_sumsq(x))
    _accumulate_ss(ss_ref, pl.program_id(1), parts)


def _norm_out(m, n, tm, tn):
    shapes = (jax.ShapeDtypeStruct((m, n), jnp.float32),
              jax.ShapeDtypeStruct((m, n), MXU_DTYPE),
              jax.ShapeDtypeStruct((m, LANES), jnp.float32))
    specs = (pl.BlockSpec((tm, tn), lambda i, j: (i, j)),
             pl.BlockSpec((tm, tn), lambda i, j: (i, j)),
             pl.BlockSpec((tm, LANES), lambda i, j: (i, 0)))
    return shapes, specs


def _matmul2_res(a1, a2, w, res, g_next, *, tm_pref, tn_pref, name):
    m, k = a1.shape
    n = w.shape[1]
    tm, tn = _pick(m, tm_pref), _pick(n, tn_pref)
    shapes, specs = _norm_out(m, n, tm, tn)
    return pl.pallas_call(
        _mm2_res_kernel,
        out_shape=shapes,
        grid=(m // tm, n // tn),
        in_specs=[pl.BlockSpec((tm, k), lambda i, j: (i, 0)),
                  pl.BlockSpec((tm, k), lambda i, j: (i, 0)),
                  pl.BlockSpec((k, tn), lambda i, j: (0, j)),
                  pl.BlockSpec((k, tn), lambda i, j: (1, j)),
                  pl.BlockSpec((tm, tn), lambda i, j: (i, j)),
                  pl.BlockSpec((1, tn), lambda i, j: (0, j))],
        out_specs=specs,
        compiler_params=_cparams("parallel", "arbitrary"),
        name=name,
    )(a1, a2, w, w, res, g_next.reshape(1, n))


def _matmul_nt(a, ss, wt_all, layer, *, tm_pref, tn_pref, name):
    m, k = a.shape
    n = wt_all.shape[1]
    tm, tn = _pick(m, tm_pref), min(tn_pref, n)
    return pl.pallas_call(
        _mm_nt_kernel,
        out_shape=jax.ShapeDtypeStruct((m, n), jnp.float32),
        grid=(m // tm, pl.cdiv(n, tn)),
        in_specs=[pl.BlockSpec((tm, k), lambda i, j: (i, 0)),
                  pl.BlockSpec((tm, LANES), lambda i, j: (i, 0)),
                  pl.BlockSpec((None, tn, k), lambda i, j: (layer, j, 0))],
        out_specs=pl.BlockSpec((tm, tn), lambda i, j: (i, j)),
        compiler_params=_cparams("parallel", "arbitrary"),
        name=name,
    )(a, ss, wt_all)


def _matmul_res(a, w, res, g_next=None, *, tm_pref, tn_pref, name):
    m, k = a.shape
    n = w.shape[1]
    tm, tn = _pick(m, tm_pref), _pick(n, tn_pref)
    in_specs = [pl.BlockSpec((tm, k), lambda i, j: (i, 0)),
                pl.BlockSpec((k, tn), lambda i, j: (0, j)),
                pl.BlockSpec((tm, tn), lambda i, j: (i, j))]
    args = [a, w, res]
    if g_next is None:
        shapes = jax.ShapeDtypeStruct((m, n), jnp.float32)
        specs = pl.BlockSpec((tm, tn), lambda i, j: (i, j))
    else:
        shapes, specs = _norm_out(m, n, tm, tn)
        in_specs.append(pl.BlockSpec((1, tn), lambda i, j: (0, j)))
        args.append(g_next.reshape(1, n))
    return pl.pallas_call(
        functools.partial(_mm_res_kernel, emit_norm=g_next is not None),
        out_shape=shapes,
        grid=(m // tm, n // tn),
        in_specs=in_specs,
        out_specs=specs,
        compiler_params=_cparams("parallel", "arbitrary"),
        name=name,
    )(*args)


def _seg_ones(seg):
    r = lax.broadcasted_iota(jnp.int32, (LANES, LANES), 0) // seg
    c = lax.broadcasted_iota(jnp.int32, (LANES, LANES), 1) // seg
    return (r == c).astype(MXU_DTYPE)


def _seg_rms(x, ones, seg):
    sq = x * x
    hi = sq.astype(MXU_DTYPE)
    lo = (sq - hi.astype(jnp.float32)).astype(MXU_DTYPE)
    ss = _dot(hi, ones) + _dot(lo, ones)
    return x * lax.rsqrt(ss * (1.0 / seg) + EPS)


def _prep_kernel(p_ref, gaq_ref, gak_ref, gbq_ref, gbk_ref,
                 qa2_ref, ka_ref, va_ref, qb_ref, kb_ref, vb_ref, qi2_ref, ki2_ref):
    ones64 = _seg_ones(A_QK_DIM)
    ones128 = _seg_ones(B_HEAD_DIM)
    lane = lax.broadcasted_iota(jnp.int32, (1, LANES), 1)
    lo_half = lane < A_QK_DIM
    gaq, gak, gbq, gbk = gaq_ref[...], gak_ref[...], gbq_ref[...], gbk_ref[...]
    a_scale = A_QK_DIM ** -0.5 * LOG2E
    b_scale = B_HEAD_DIM ** -0.5 * LOG2E
    i_scale = IDX_DIM ** -0.5

    for h in range(A_HEADS):
        q = _seg_rms(p_ref[0, :, OFF_AQ + h * LANES:OFF_AQ + (h + 1) * LANES], ones64, A_QK_DIM)
        q = q * gaq * a_scale
        qa2_ref[0, h, 0] = jnp.where(lo_half, q, 0.0).astype(qa2_ref.dtype)
        qa2_ref[0, h, 1] = jnp.where(lo_half, 0.0, q).astype(qa2_ref.dtype)
        k = _seg_rms(p_ref[0, :, OFF_AK + h * LANES:OFF_AK + (h + 1) * LANES], ones64, A_QK_DIM)
        ka_ref[0, h] = (k * gak).astype(ka_ref.dtype)
        va_ref[0, h] = p_ref[0, :, OFF_AV + h * LANES:OFF_AV + (h + 1) * LANES].astype(va_ref.dtype)
    for h in range(B_HEADS):
        q = _seg_rms(p_ref[0, :, OFF_BQ + h * LANES:OFF_BQ + (h + 1) * LANES], ones128, B_HEAD_DIM)
        qb_ref[0, h] = (q * gbq * b_scale).astype(qb_ref.dtype)
    k = _seg_rms(p_ref[0, :, OFF_BK:OFF_BK + B_K], ones128, B_HEAD_DIM)
    kb_ref[0] = (k * gbk).astype(kb_ref.dtype)
    vb_ref[0] = p_ref[0, :, OFF_BV:OFF_BV + B_V].astype(vb_ref.dtype)
    for hp in range(IDX_HEADS // 2):
        qi = p_ref[0, :, OFF_IQ + hp * LANES:OFF_IQ + (hp + 1) * LANES] * i_scale
        qi2_ref[0, 2 * hp] = jnp.where(lo_half, qi, 0.0).astype(qi2_ref.dtype)
        qi2_ref[0, 2 * hp + 1] = jnp.where(lo_half, 0.0, qi).astype(qi2_ref.dtype)
    kt = p_ref[0, :, OFF_IK:OFF_IK + I_K]
    ki2_ref[0] = jnp.concatenate([kt, kt], axis=-1).astype(ki2_ref.dtype)


def _prep(proj3, gaq, gak, gbq, gbk):
    b, s, npad = proj3.shape
    ts = _pick(s, 256)
    dt = MXU_DTYPE
    head4 = lambda nh: jax.ShapeDtypeStruct((b, nh, s, LANES), dt)
    flat3 = jax.ShapeDtypeStruct((b, s, LANES), dt)
    hspec = lambda nh: pl.BlockSpec((1, nh, ts, LANES), lambda bi, i: (bi, 0, i, 0))
    fspec = pl.BlockSpec((1, ts, LANES), lambda bi, i: (bi, i, 0))
    gspec = pl.BlockSpec((1, LANES), lambda bi, i: (0, 0))
    return pl.pallas_call(
        _prep_kernel,
        out_shape=(jax.ShapeDtypeStruct((b, A_HEADS, 2, s, LANES), dt),
                   head4(A_HEADS), head4(A_HEADS), head4(B_HEADS), flat3, flat3,
                   head4(IDX_HEADS), flat3),
        grid=(b, s // ts),
        in_specs=[pl.BlockSpec((1, ts, npad), lambda bi, i: (bi, i, 0)), gspec, gspec, gspec, gspec],
        out_specs=(pl.BlockSpec((1, A_HEADS, 2, ts, LANES), lambda bi, i: (bi, 0, 0, i, 0)),
                   hspec(A_HEADS), hspec(A_HEADS), hspec(B_HEADS), fspec, fspec,
                   hspec(IDX_HEADS), fspec),
        compiler_params=_cparams("parallel", "parallel"),
        name="head_prep",
    )(proj3, gaq, gak, gbq, gbk)


def _rel_bucket_np(rel):
    nb = N_BUCKETS // 2
    max_exact = nb // 2
    bucket = np.where(rel > 0, nb, 0)
    n = np.abs(rel)
    nf = np.maximum(n, 1).astype(np.float32)
    large = max_exact + (np.log(nf / np.float32(max_exact)) / np.float32(math.log(MAX_DISTANCE / max_exact))
                         * np.float32(nb - max_exact)).astype(np.int32)
    large = np.minimum(large, nb - 1)
    return (bucket + np.where(n < max_exact, n, large)).astype(np.int32)


def _near_bucket_ids(t):
    tq = np.arange(t)[:, None]
    out = []
    for key0 in (0, -t):
        ts = key0 + np.arange(2 * t)[None, :]
        ids = _rel_bucket_np(ts - tq)
        vis = np.floor_divide(ts, CHUNK) <= np.floor_divide(tq, CHUNK)
        out.append(np.where(vis, ids, -1))
    return np.stack(out).astype(np.int32)


FAR_BUCKET = N_BUCKETS // 2 - 1


def _bias_kernel(tab_ref, ids_ref, o_ref):
    h = pl.program_id(0)
    ids = ids_ref[...]
    far = tab_ref[FAR_BUCKET, h]
    acc = jnp.full(ids.shape, NEG_INF, jnp.float32)
    for bkt in range(N_BUCKETS):
        acc = jnp.where(ids == bkt, (tab_ref[bkt, h] - far) * LOG2E, acc)
    o_ref[0] = acc


def _near_bias(rel_bias, t):
    ids = jnp.asarray(_near_bucket_ids(t))
    nh = rel_bias.shape[1]
    return pl.pallas_call(
        _bias_kernel,
        out_shape=jax.ShapeDtypeStruct((nh, 2, t, 2 * t), jnp.float32),
        grid=(nh,),
        in_specs=[pl.BlockSpec(memory_space=pltpu.SMEM),
                  pl.BlockSpec((2, t, 2 * t), lambda h: (0, 0, 0))],
        out_specs=pl.BlockSpec((1, 2, t, 2 * t), lambda h: (h, 0, 0, 0)),
        compiler_params=_cparams("arbitrary"),
        name="near_bias",
    )(rel_bias, ids)


def _softmax_step(s, v, m_sc, l_sc, acc_sc, idx, first):
    rows, width = s.shape
    blocks = [s[:, c * LANES:(c + 1) * LANES] for c in range(width // LANES)]
    m_part = functools.reduce(jnp.maximum, blocks)
    m_cur = jnp.broadcast_to(jnp.max(m_part, axis=-1, keepdims=True), (rows, LANES))
    if first:
        m_new = jnp.maximum(m_cur, M_FLOOR)
    else:
        m_prev = m_sc[idx]
        m_new = jnp.maximum(m_prev, m_cur)
        alpha = jnp.exp2(m_prev - m_new)
    ps = [jnp.exp2(blk - m_new) for blk in blocks]
    l_part = functools.reduce(jnp.add, ps)
    l_cur = jnp.broadcast_to(jnp.sum(l_part, axis=-1, keepdims=True), (rows, LANES))
    p = jnp.concatenate([x.astype(v.dtype) for x in ps], axis=-1)
    pv = _dot(p, v)
    if first:
        l_sc[idx] = l_cur
        acc_sc[idx] = pv
    else:
        l_sc[idx] = alpha * l_sc[idx] + l_cur
        acc_sc[idx] = alpha * acc_sc[idx] + pv
    m_sc[idx] = m_new


def _diff_kernel(q2_ref, k_ref, v_ref, bias_ref, lam_ref, g_ref, o_ref, m_sc, l_sc, acc_sc,
                 *, lam_init, t, hg):
    i = pl.program_id(2)
    n_far = jnp.maximum(i - 1, 0)
    near0 = pl.multiple_of(n_far * t, t)

    def step(k0, width, first):
        for hh in range(hg):
            q2 = q2_ref[0, hh].reshape(2 * t, LANES)
            s = _dot_nt(q2, k_ref[0, hh, pl.ds(k0, width), :])
            if first:
                s = (s.reshape(2, t, width) + bias_ref[hh, 0][None]).reshape(2 * t, width)
            _softmax_step(s, v_ref[0, hh, pl.ds(k0, width), :], m_sc, l_sc, acc_sc, hh, first)

    step(near0, 2 * t, True)

    def far_body(j, carry):
        step(pl.multiple_of(j * 2 * t, 2 * t), 2 * t, False)
        return carry

    lax.fori_loop(0, n_far // 2, far_body, 0)

    @pl.when(n_far % 2 == 1)
    def _():
        step(pl.multiple_of((n_far - 1) * t, t), t, False)

    lq = lam_ref[...]
    lam = (jnp.exp(jnp.sum(lq[0:1] * lq[1:2], axis=-1, keepdims=True))
           - jnp.exp(jnp.sum(lq[2:3] * lq[3:4], axis=-1, keepdims=True)) + lam_init)
    for hh in range(hg):
        o = acc_sc[hh] / l_sc[hh]
        o = o[:t] - lam * o[t:]
        ms = jnp.mean(o * o, axis=-1, keepdims=True)
        o = o * lax.rsqrt(ms + EPS) * g_ref[...]
        o_ref[:, hh * A_V_DIM:(hh + 1) * A_V_DIM] = (o * (1.0 - lam_init)).astype(o_ref.dtype)


def _diff_attention(qa2, ka, va, bias, lam_qk, g_out, lam_init):
    b, nh, _, s, _ = qa2.shape
    t = DIFF_T
    hg = DIFF_HEADS_PER_STEP
    nt = s // t
    rows = 2 * t
    return pl.pallas_call(
        functools.partial(_diff_kernel, lam_init=lam_init, t=t, hg=hg),
        out_shape=jax.ShapeDtypeStruct((b * s, nh * A_V_DIM), MXU_DTYPE),
        grid=(b, nh // hg, nt),
        in_specs=[pl.BlockSpec((1, hg, 2, t, LANES), lambda bi, h, i: (bi, h, 0, i, 0)),
                  pl.BlockSpec((1, hg, s, LANES), lambda bi, h, i: (bi, h, 0, 0)),
                  pl.BlockSpec((1, hg, s, LANES), lambda bi, h, i: (bi, h, 0, 0)),
                  pl.BlockSpec((hg, 1, t, 2 * t), lambda bi, h, i: (h, jnp.minimum(i, 1), 0, 0)),
                  pl.BlockSpec((4, A_QK_DIM), lambda bi, h, i: (0, 0)),
                  pl.BlockSpec((1, A_V_DIM), lambda bi, h, i: (0, 0))],
        out_specs=pl.BlockSpec((t, hg * A_V_DIM), lambda bi, h, i: (bi * nt + i, h)),
        scratch_shapes=[pltpu.VMEM((hg, rows, LANES), jnp.float32),
                        pltpu.VMEM((hg, rows, LANES), jnp.float32),
                        pltpu.VMEM((hg, rows, A_V_DIM), jnp.float32)],
        compiler_params=_cparams("parallel", "parallel", "arbitrary"),
        name="diff_attention",
    )(qa2, ka, va, bias, lam_qk, g_out.reshape(1, A_V_DIM))


def _sortable_key(x):
    bits = pltpu.bitcast(x, jnp.int32)
    return bits ^ ((bits >> 31) & 0x7FFFFFFF)


def _dsa_kernel(qi2_ref, ki2_ref, wi_ref, qb_ref, kb_ref, vb_ref, bias_ref, wo_ref, o_ref, woc_ref,
                key_sc, hi_sc, lo_sc, am_sc, m_sc, l_sc, acc_sc, *, topk, heads_per_group):
    woc_ref[...] = wo_ref[...].astype(woc_ref.dtype)
    t = ATT_T
    kb_sz = IDX_KB
    i = pl.program_id(1)
    q0 = i * t
    n_idx_blocks = (jnp.maximum(q0 + t, 2 * t) + kb_sz - 1) // kb_sz
    t_idx = q0 + lax.broadcasted_iota(jnp.int32, (1, t), 1)
    t_chunk = t_idx >> CHUNK_SHIFT

    wi = wi_ref[0] * (IDX_HEADS ** -0.5)

    def score_body(kbi, carry):
        k0 = pl.multiple_of(kbi * kb_sz, kb_sz)
        kblk = ki2_ref[0, pl.ds(k0, kb_sz), :]
        sc = jnp.zeros((kb_sz, t), jnp.float32)
        for h in range(IDX_HEADS):
            d = _dot_nt(kblk, qi2_ref[0, h])
            sc = sc + jnp.maximum(d, 0.0) * wi[h:h + 1, :]
        s_chunk = (k0 + lax.broadcasted_iota(jnp.int32, (kb_sz, 1), 0)) >> CHUNK_SHIFT
        sc = jnp.where(s_chunk <= t_chunk, sc, NEG_INF)
        key = _sortable_key(sc)
        key_sc[pl.ds(k0, kb_sz), :] = key
        hi_sc[pl.ds(k0, kb_sz), :] = (key >> 16).astype(jnp.int16)
        lo_sc[pl.ds(k0, kb_sz), :] = ((key & 0xFFFF) + I16_MIN).astype(jnp.int16)
        return carry

    lax.fori_loop(0, n_idx_blocks, score_body, 0)

    kk = jnp.minimum((t_chunk + 1) * CHUNK, topk)

    def count16(ref, cand, strict):
        cand16 = cand.astype(jnp.int16)

        def body(kbi, c):
            k0 = pl.multiple_of(kbi * kb_sz, kb_sz)
            blk = ref[pl.ds(k0, kb_sz), :]
            hit = (blk > cand16) if strict else (blk >= cand16)
            h3 = hit.astype(jnp.int16).reshape(kb_sz // 16, 16, t)
            for r in range(kb_sz // 16):
                c = c + h3[r]
            return c

        c16 = lax.fori_loop(0, n_idx_blocks, body, jnp.zeros((16, t), jnp.int16))
        return jnp.sum(c16.astype(jnp.int32), axis=0, keepdims=True)

    def search16(ref, base):
        def bit_body(it, thr):
            cand = thr + jnp.left_shift(jnp.int32(1), 15 - it)
            return jnp.where(base + count16(ref, cand, False) >= kk, cand, thr)
        return lax.fori_loop(0, 16, bit_body, jnp.full((1, t), I16_MIN, jnp.int32))

    thr_hi = search16(hi_sc, jnp.zeros((1, t), jnp.int32))
    above = count16(hi_sc, thr_hi, True)
    thr_hi16 = thr_hi.astype(jnp.int16)

    def tie_body(kbi, carry):
        k0 = pl.multiple_of(kbi * kb_sz, kb_sz)
        same = hi_sc[pl.ds(k0, kb_sz), :] == thr_hi16
        lo_sc[pl.ds(k0, kb_sz), :] = jnp.where(same, lo_sc[pl.ds(k0, kb_sz), :], jnp.int16(I16_MIN))
        return carry

    lax.fori_loop(0, n_idx_blocks, tie_body, 0)
    thr_lo = search16(lo_sc, above)
    thr = (thr_hi << 16) + (thr_lo - I16_MIN)
    n_greater = above + count16(lo_sc, thr_lo, True)
    slots = (kk - n_greater).astype(jnp.float32)

    tri = (lax.broadcasted_iota(jnp.int32, (kb_sz, kb_sz), 0)
           >= lax.broadcasted_iota(jnp.int32, (kb_sz, kb_sz), 1)).astype(MXU_DTYPE)

    def mask_body(kbi, ties_before):
        k0 = pl.multiple_of(kbi * kb_sz, kb_sz)
        key = key_sc[pl.ds(k0, kb_sz), :]
        tie = key == thr
        rank = ties_before + _dot(tri, tie.astype(MXU_DTYPE))
        sel = (key > thr) | (tie & (rank <= slots))
        am_sc[:, pl.ds(k0, kb_sz)] = jnp.where(sel, 0.0, NEG_INF).T
        return rank[kb_sz - 1:kb_sz, :]

    def plain_mask_body(kbi, carry):
        k0 = pl.multiple_of(kbi * kb_sz, kb_sz)
        am_sc[:, pl.ds(k0, kb_sz)] = jnp.where(key_sc[pl.ds(k0, kb_sz), :] >= thr, 0.0, NEG_INF).T
        return carry

    surplus = jnp.max(above + count16(lo_sc, thr_lo, False) - kk)

    @pl.when(surplus > 0)
    def _():
        lax.fori_loop(0, n_idx_blocks, mask_body, jnp.zeros((1, t), jnp.float32))

    @pl.when(surplus <= 0)
    def _():
        lax.fori_loop(0, n_idx_blocks, plain_mask_body, 0)

    g = heads_per_group
    n_groups = B_HEADS // g
    far_w = DSA_FAR_W
    near0 = pl.multiple_of(jnp.maximum(i - 1, 0) * t, t)
    am_near = am_sc[:, pl.ds(near0, 2 * t)]
    am_sc[:, pl.ds(near0, far_w + 2 * t)] = jnp.full((t, far_w + 2 * t), NEG_INF, jnp.float32)

    def step(k0, width, first):
        kblk = kb_ref[0, pl.ds(k0, width), :]
        vblk = vb_ref[0, pl.ds(k0, width), :]
        for hg in range(n_groups):
            q = qb_ref[0, hg * g:(hg + 1) * g].reshape(g * t, LANES)
            s = _dot_nt(q, kblk).reshape(g, t, width)
            if first:
                s = s + (bias_ref[hg * g:(hg + 1) * g, 0] + am_near[None])
            else:
                s = s + am_sc[:, pl.ds(k0, width)][None]
            _softmax_step(s.reshape(g * t, width), vblk, m_sc, l_sc, acc_sc, hg, first)

    step(near0, 2 * t, True)

    def far_body(j, carry):
        step(pl.multiple_of(j * far_w, far_w), far_w, False)
        return carry

    lax.fori_loop(0, (near0 + far_w - 1) // far_w, far_body, 0)

    for hg in range(n_groups):
        o = acc_sc[hg] / l_sc[hg]
        for hh in range(g):
            h = hg * g + hh
            o_ref[:, h * B_HEAD_DIM:(h + 1) * B_HEAD_DIM] = o[hh * t:(hh + 1) * t].astype(o_ref.dtype)


def _dsa_attention(qi2, ki2, wi_t, qb, kb, vb, bias, w_out_all, layer, topk):
    b, nh, s, _ = qb.shape
    t = ATT_T
    nt = s // t
    g = DSA_HEADS_PER_GROUP
    n_groups = nh // g
    nb_cols = B_HEADS * B_HEAD_DIM
    _, wo_rows, wo_cols = w_out_all.shape
    slab = wo_rows // (b * nt)
    assert slab * b * nt == wo_rows and slab % 16 == 0
    return pl.pallas_call(
        functools.partial(_dsa_kernel, topk=topk, heads_per_group=g),
        out_shape=(jax.ShapeDtypeStruct((b * s, nb_cols), MXU_DTYPE),
                   jax.ShapeDtypeStruct((wo_rows, wo_cols), MXU_DTYPE)),
        grid=(b, nt),
        in_specs=[pl.BlockSpec((1, IDX_HEADS, t, LANES), lambda bi, i: (bi, 0, i, 0)),
                  pl.BlockSpec((1, s, LANES), lambda bi, i: (bi, 0, 0)),
                  pl.BlockSpec((1, IDX_HEADS, t), lambda bi, i: (bi, 0, i)),
                  pl.BlockSpec((1, nh, t, LANES), lambda bi, i: (bi, 0, i, 0)),
                  pl.BlockSpec((1, s, LANES), lambda bi, i: (bi, 0, 0)),
                  pl.BlockSpec((1, s, LANES), lambda bi, i: (bi, 0, 0)),
                  pl.BlockSpec((B_HEADS, 1, t, 2 * t), lambda bi, i: (0, jnp.minimum(i, 1), 0, 0)),
                  pl.BlockSpec((None, slab, wo_cols), lambda bi, i: (layer, bi * nt + i, 0))],
        out_specs=(pl.BlockSpec((t, nb_cols), lambda bi, i: (bi * nt + i, 0)),
                   pl.BlockSpec((slab, wo_cols), lambda bi, i: (bi * nt + i, 0))),
        scratch_shapes=[pltpu.VMEM((s, t), jnp.int32),
                        pltpu.VMEM((s, t), jnp.int16),
                        pltpu.VMEM((s, t), jnp.int16),
                        pltpu.VMEM((t, s + DSA_FAR_W), jnp.float32),
                        pltpu.VMEM((n_groups, g * t, LANES), jnp.float32),
                        pltpu.VMEM((n_groups, g * t, LANES), jnp.float32),
                        pltpu.VMEM((n_groups, g * t, B_HEAD_DIM), jnp.float32)],
        compiler_params=_cparams("arbitrary", "arbitrary"),
        name="dsa_attention",
    )(qi2, ki2, wi_t, qb, kb, vb, bias, w_out_all)


def _ffn1_kernel(h_ref, ss_ref, hp_ref, ssp_ref, wg_ref, wu_ref, cw_ref, cb_ref, wd_ref,
                 o_ref, wdc_ref, *, tiles_per_seq):
    wdc_ref[...] = wd_ref[...].astype(wdc_ref.dtype)
    i = pl.program_id(0)
    tm, d = h_ref.shape
    h = h_ref[...]
    rs = _row_scale(ss_ref, d)
    wg = wg_ref[...].astype(MXU_DTYPE)
    gate = _dot(h, wg) * rs
    up = _dot(h, wu_ref[...].astype(MXU_DTYPE)) * (0.5 * rs)
    prev = _dot(hp_ref[...], wg) * _row_scale(ssp_ref, d)
    prev = jnp.where(i % tiles_per_seq == 0, 0.0, prev)
    cw = cw_ref[...]
    cb = cb_ref[...]

    def glu(g2, g1, g0, u):
        gc = cw[0:1] * g2 + cw[1:2] * g1 + cw[2:3] * g0 + cb
        return (gc * (1.0 + jnp.tanh(0.5 * gc)) * u).astype(o_ref.dtype)

    o_ref[...] = glu(pltpu.roll(gate, 2, 0), pltpu.roll(gate, 1, 0), gate, up)
    top = gate[0:8]
    row = lax.broadcasted_iota(jnp.int32, (8, 1), 0)
    t1 = jnp.where(row == 0, prev[7:8], pltpu.roll(top, 1, 0))
    t2 = jnp.where(row == 0, prev[6:7], jnp.where(row == 1, prev[7:8], pltpu.roll(top, 2, 0)))
    o_ref[0:8, :] = glu(t2, t1, top, up[0:8])


def _ffn1(h, ss, w_gu_all, conv_w_all, conv_b_all, w_down_all, layer, seq):
    m, d = h.shape
    dff = conv_w_all.shape[2]
    d_out = w_down_all.shape[2]
    tm = _pick(seq, 1024)
    tn = _pick(dff, 256)
    nj = dff // tn
    sub = 8
    n_steps = (m // tm) * nj
    slab = dff // n_steps
    assert slab * n_steps == dff and slab % 16 == 0
    return pl.pallas_call(
        functools.partial(_ffn1_kernel, tiles_per_seq=seq // tm),
        out_shape=(jax.ShapeDtypeStruct((m, dff), MXU_DTYPE),
                   jax.ShapeDtypeStruct((dff, d_out), MXU_DTYPE)),
        grid=(m // tm, nj),
        in_specs=[pl.BlockSpec((tm, d), lambda i, j: (i, 0)),
                  pl.BlockSpec((tm, LANES), lambda i, j: (i, 0)),
                  pl.BlockSpec((sub, d), lambda i, j: (jnp.maximum(i * (tm // sub) - 1, 0), 0)),
                  pl.BlockSpec((sub, LANES), lambda i, j: (jnp.maximum(i * (tm // sub) - 1, 0), 0)),
                  pl.BlockSpec((None, d, tn), lambda i, j: (layer, 0, j)),
                  pl.BlockSpec((None, d, tn), lambda i, j: (layer, 0, j + nj)),
                  pl.BlockSpec((None, CONV_WIDTH, tn), lambda i, j: (layer, 0, j)),
                  pl.BlockSpec((None, 1, tn), lambda i, j: (layer, 0, j)),
                  pl.BlockSpec((None, slab, d_out), lambda i, j: (layer, i * nj + j, 0))],
        out_specs=(pl.BlockSpec((tm, tn), lambda i, j: (i, j)),
                   pl.BlockSpec((slab, d_out), lambda i, j: (i * nj + j, 0))),
        compiler_params=_cparams("arbitrary", "arbitrary"),
        name="ffn_gate_up_glu",
    )(h, ss, h, ss, w_gu_all, w_gu_all, conv_w_all, conv_b_all.reshape(-1, 1, dff), w_down_all)


def kernel(x, attn_norm, w_in, a_q_norm, a_k_norm, lambda_qk, a_out_norm, b_q_norm, b_k_norm,
           rel_bias, w_out, ffn_norm, w_gate_up, conv_w, conv_b, w_down):
    bsz, s_len, d_model = x.shape
    depth = w_in.shape[0]
    m = bsz * s_len
    topk = min(TOPK_MAX, s_len // 4)
    assert s_len % (2 * DIFF_T) == 0 and s_len % DSA_FAR_W == 0 and s_len % IDX_KB == 0
    assert w_in.shape[2] == D_IN and w_out.shape[1] == A_V + B_Q

    near_bias_a = _near_bias(rel_bias[:, :A_HEADS], DIFF_T)
    near_bias_b = _near_bias(rel_bias[:, A_HEADS:], ATT_T)
    x2 = x.reshape(m, d_model)
    w_in_t = jnp.swapaxes(w_in, 1, 2)
    xg, ss = _norm_inputs(x2, attn_norm[0])
    for l in range(depth):
        lam_init = 0.8 - 0.6 * math.exp(-0.3 * l)
        proj = _matmul_nt(xg, ss, w_in_t, l, tm_pref=1024, tn_pref=512, name="in_proj")
        proj3 = proj.reshape(bsz, s_len, D_IN)
        two = lambda g: jnp.concatenate([g, g]).reshape(1, LANES)
        qa2, ka, va, qb, kb, vb, qi2, ki2 = _prep(
            proj3, two(a_q_norm[l]), two(a_k_norm[l]),
            b_q_norm[l].reshape(1, LANES), b_k_norm[l].reshape(1, LANES))
        wi_t = jnp.swapaxes(proj3[:, :, OFF_IW:OFF_IW + I_W], 1, 2)
        mix_a = _diff_attention(qa2, ka, va, near_bias_a, lambda_qk[l], a_out_norm[l], lam_init)
        mix_b, w_out_c = _dsa_attention(qi2, ki2, wi_t, qb, kb, vb, near_bias_b, w_out, l, topk)
        x2, xg, ss = _matmul2_res(mix_a, mix_b, w_out_c, x2, ffn_norm[l],
                                  tm_pref=1024, tn_pref=512, name="out_proj")
        act, w_down_c = _ffn1(xg, ss, w_gate_up, conv_w, conv_b, w_down, l, s_len)
        if l + 1 < depth:
            x2, xg, ss = _matmul_res(act, w_down_c, x2, attn_norm[l + 1],
                                     tm_pref=512, tn_pref=512, name="down_proj")
        else:
            x2 = _matmul_res(act, w_down_c, x2, tm_pref=512, tn_pref=512, name="down_proj")
    return x2.reshape(bsz, s_len, d_model)
```

```python
import functools
import math

import numpy as np
import jax
import jax.numpy as jnp
from jax import lax
from jax.experimental import pallas as pl
from jax.experimental.pallas import tpu as pltpu

CHUNK = 64
CHUNK_SHIFT = 6
A_HEADS = 16
A_QK_DIM = 64
A_V_DIM = 128
B_HEADS = 16
B_HEAD_DIM = 128
IDX_HEADS = 16
IDX_DIM = 64
TOPK_MAX = 256
N_BUCKETS = 32
MAX_DISTANCE = 128
CONV_WIDTH = 3
EPS = 1e-6

A_Q = A_HEADS * 2 * A_QK_DIM
A_K = A_HEADS * 2 * A_QK_DIM
A_V = A_HEADS * A_V_DIM
B_Q = B_HEADS * B_HEAD_DIM
B_K = B_HEAD_DIM
B_V = B_HEAD_DIM
I_Q = IDX_HEADS * IDX_DIM
I_K = IDX_DIM
I_W = IDX_HEADS
OFF_AQ = 0
OFF_AK = OFF_AQ + A_Q
OFF_AV = OFF_AK + A_K
OFF_BQ = OFF_AV + A_V
OFF_BK = OFF_BQ + B_Q
OFF_BV = OFF_BK + B_K
OFF_IQ = OFF_BV + B_V
OFF_IK = OFF_IQ + I_Q
OFF_IW = OFF_IK + I_K
D_IN = OFF_IW + I_W

LANES = 128
MXU_N = 256
V7X_VMEM_LIMIT = 58 * 1024 * 1024

DIFF_T = 256
DIFF_HEADS_PER_STEP = 8
ATT_T = 256
IDX_KB = 256
DSA_FAR_W = 256
DSA_HEADS_PER_GROUP = 2
MXU_DTYPE = jnp.bfloat16

LOG2E = math.log2(math.e)
NEG_INF = float("-inf")
M_FLOOR = -1e30
I16_MIN = -(2 ** 15)


def _cparams(*sem):
    return pltpu.CompilerParams(dimension_semantics=sem, vmem_limit_bytes=V7X_VMEM_LIMIT)


def _pick(n, pref):
    t = min(pref, n)
    while n % t:
        t -= LANES
    return t


def _dot(a, b):
    return jnp.dot(a, b, preferred_element_type=jnp.float32)


def _dot_nt(a, b):
    return lax.dot_general(a, b, (((1,), (1,)), ((), ())), preferred_element_type=jnp.float32)


def _row_sumsq(x):
    sq = functools.reduce(jnp.add, [x[:, c * LANES:(c + 1) * LANES] ** 2
                                    for c in range(x.shape[1] // LANES)])
    return jnp.broadcast_to(jnp.sum(sq, axis=-1, keepdims=True), (x.shape[0], LANES))


def _row_scale(ss_ref, d):
    return lax.rsqrt(ss_ref[:, 0:1] * (1.0 / d) + EPS)


def _col_chains(width):
    w = MXU_N if width % MXU_N == 0 else width
    return [slice(c, c + w) for c in range(0, width, w)]


def _accumulate_ss(ss_ref, j, parts):
    @pl.when(j == 0)
    def _():
        ss_ref[...] = jnp.zeros(ss_ref.shape, jnp.float32)

    ss_ref[...] += functools.reduce(jnp.add, parts)


def _norm_inputs_kernel(x_ref, g_ref, xg_ref, ss_ref):
    x = x_ref[...]
    xg_ref[...] = (x * g_ref[...]).astype(xg_ref.dtype)
    ss_ref[...] = _row_sumsq(x)


def _norm_inputs(x2d, g):
    m, d = x2d.shape
    tm = _pick(m, 256)
    return pl.pallas_call(
        _norm_inputs_kernel,
        out_shape=(jax.ShapeDtypeStruct((m, d), MXU_DTYPE),
                   jax.ShapeDtypeStruct((m, LANES), jnp.float32)),
        grid=(m // tm,),
        in_specs=[pl.BlockSpec((tm, d), lambda i: (i, 0)),
                  pl.BlockSpec((1, d), lambda i: (0, 0))],
        out_specs=(pl.BlockSpec((tm, d), lambda i: (i, 0)),
                   pl.BlockSpec((tm, LANES), lambda i: (i, 0))),
        compiler_params=_cparams("parallel"),
        name="norm_inputs",
    )(x2d, g.reshape(1, d))


def _mm_nt_kernel(a_ref, ss_ref, wt_ref, o_ref):
    a = a_ref[...]
    rs = _row_scale(ss_ref, a.shape[1])
    for cols in _col_chains(o_ref.shape[1]):
        acc = _dot_nt(a, wt_ref[cols, :].astype(MXU_DTYPE))
        o_ref[:, cols] = (acc * rs).astype(o_ref.dtype)


def _mm_res_kernel(a_ref, w_ref, r_ref, *rest, emit_norm):
    if emit_norm:
        g_ref, o_ref, xg_ref, ss_ref = rest
    else:
        (o_ref,) = rest
    a = a_ref[...]
    parts = []
    for cols in _col_chains(o_ref.shape[1]):
        x = r_ref[:, cols] + _dot(a, w_ref[:, cols])
        o_ref[:, cols] = x
        if emit_norm:
            xg_ref[:, cols] = (x * g_ref[:, cols]).astype(xg_ref.dtype)
            parts.append(_row_sumsq(x))
    if emit_norm:
        _accumulate_ss(ss_ref, pl.program_id(1), parts)


def _mm2_res_kernel(a1_ref, a2_ref, w1_ref, w2_ref, r_ref, g_ref, o_ref, xg_ref, ss_ref):
    a1, a2 = a1_ref[...], a2_ref[...]
    parts = []
    for cols in _col_chains(o_ref.shape[1]):
        acc = (_dot(a1, w1_ref[:, cols].astype(MXU_DTYPE))
               + _dot(a2, w2_ref[:, cols].astype(MXU_DTYPE)))
        x = r_ref[:, cols] + acc
        o_ref[:, cols] = x
        xg_ref[:, cols] = (x * g_ref[:, cols]).astype(xg_ref.dtype)
        parts.append(_row_sumsq(x))
    _accumulate_ss(ss_ref, pl.program_id(1), parts)


def _norm_out(m, n, tm, tn):
    shapes = (jax.ShapeDtypeStruct((m, n), jnp.float32),
              jax.ShapeDtypeStruct((m, n), MXU_DTYPE),
              jax.ShapeDtypeStruct((m, LANES), jnp.float32))
    specs = (pl.BlockSpec((tm, tn), lambda i, j: (i, j)),
             pl.BlockSpec((tm, tn), lambda i, j: (i, j)),
             pl.BlockSpec((tm, LANES), lambda i, j: (i, 0)))
    return shapes, specs


def _matmul2_res(a1, a2, w, res, g_next, *, tm_pref, tn_pref, name):
    m, k = a1.shape
    n = w.shape[1]
    tm, tn = _pick(m, tm_pref), _pick(n, tn_pref)
    shapes, specs = _norm_out(m, n, tm, tn)
    return pl.pallas_call(
        _mm2_res_kernel,
        out_shape=shapes,
        grid=(m // tm, n // tn),
        in_specs=[pl.BlockSpec((tm, k), lambda i, j: (i, 0)),
                  pl.BlockSpec((tm, k), lambda i, j: (i, 0)),
                  pl.BlockSpec((k, tn), lambda i, j: (0, j)),
                  pl.BlockSpec((k, tn), lambda i, j: (1, j)),
                  pl.BlockSpec((tm, tn), lambda i, j: (i, j)),
                  pl.BlockSpec((1, tn), lambda i, j: (0, j))],
        out_specs=specs,
        compiler_params=_cparams("parallel", "arbitrary"),
        name=name,
    )(a1, a2, w, w, res, g_next.reshape(1, n))


def _matmul_nt(a, ss, wt_all, layer, *, tm_pref, tn_pref, name):
    m, k = a.shape
    n = wt_all.shape[1]
    tm, tn = _pick(m, tm_pref), min(tn_pref, n)
    return pl.pallas_call(
        _mm_nt_kernel,
        out_shape=jax.ShapeDtypeStruct((m, n), jnp.float32),
        grid=(m // tm, pl.cdiv(n, tn)),
        in_specs=[pl.BlockSpec((tm, k), lambda i, j: (i, 0)),
                  pl.BlockSpec((tm, LANES), lambda i, j: (i, 0)),
                  pl.BlockSpec((None, tn, k), lambda i, j: (layer, j, 0))],
        out_specs=pl.BlockSpec((tm, tn), lambda i, j: (i, j)),
        compiler_params=_cparams("parallel", "arbitrary"),
        name=name,
    )(a, ss, wt_all)


def _matmul_res(a, w, res, g_next=None, *, tm_pref, tn_pref, name):
    m, k = a.shape
    n = w.shape[1]
    tm, tn = _pick(m, tm_pref), _pick(n, tn_pref)
    in_specs = [pl.BlockSpec((tm, k), lambda i, j: (i, 0)),
                pl.BlockSpec((k, tn), lambda i, j: (0, j)),
                pl.BlockSpec((tm, tn), lambda i, j: (i, j))]
    args = [a, w, res]
    if g_next is None:
        shapes = jax.ShapeDtypeStruct((m, n), jnp.float32)
        specs = pl.BlockSpec((tm, tn), lambda i, j: (i, j))
    else:
        shapes, specs = _norm_out(m, n, tm, tn)
        in_specs.append(pl.BlockSpec((1, tn), lambda i, j: (0, j)))
        args.append(g_next.reshape(1, n))
    return pl.pallas_call(
        functools.partial(_mm_res_kernel, emit_norm=g_next is not None),
        out_shape=shapes,
        grid=(m // tm, n // tn),
        in_specs=in_specs,
        out_specs=specs,
        compiler_params=_cparams("parallel", "arbitrary"),
        name=name,
    )(*args)


def _seg_ones(seg):
    r = lax.broadcasted_iota(jnp.int32, (LANES, LANES), 0) // seg
    c = lax.broadcasted_iota(jnp.int32, (LANES, LANES), 1) // seg
    return (r == c).astype(MXU_DTYPE)


def _seg_rms(x, ones, seg):
    sq = x * x
    hi = sq.astype(MXU_DTYPE)
    lo = (sq - hi.astype(jnp.float32)).astype(MXU_DTYPE)
    ss = _dot(hi, ones) + _dot(lo, ones)
    return x * lax.rsqrt(ss * (1.0 / seg) + EPS)


def _prep_kernel(p_ref, gaq_ref, gak_ref, gbq_ref, gbk_ref,
                 qa2_ref, ka_ref, va_ref, qb_ref, kb_ref, vb_ref, qi2_ref, ki2_ref):
    ones64 = _seg_ones(A_QK_DIM)
    ones128 = _seg_ones(B_HEAD_DIM)
    lane = lax.broadcasted_iota(jnp.int32, (1, LANES), 1)
    lo_half = lane < A_QK_DIM
    gaq, gak, gbq, gbk = gaq_ref[...], gak_ref[...], gbq_ref[...], gbk_ref[...]
    a_scale = A_QK_DIM ** -0.5 * LOG2E
    b_scale = B_HEAD_DIM ** -0.5 * LOG2E
    i_scale = IDX_DIM ** -0.5

    for h in range(A_HEADS):
        q = _seg_rms(p_ref[0, :, OFF_AQ + h * LANES:OFF_AQ + (h + 1) * LANES], ones64, A_QK_DIM)
        q = q * gaq * a_scale
        qa2_ref[0, h, 0] = jnp.where(lo_half, q, 0.0).astype(qa2_ref.dtype)
        qa2_ref[0, h, 1] = jnp.where(lo_half, 0.0, q).astype(qa2_ref.dtype)
        k = _seg_rms(p_ref[0, :, OFF_AK + h * LANES:OFF_AK + (h + 1) * LANES], ones64, A_QK_DIM)
        ka_ref[0, h] = (k * gak).astype(ka_ref.dtype)
        va_ref[0, h] = p_ref[0, :, OFF_AV + h * LANES:OFF_AV + (h + 1) * LANES].astype(va_ref.dtype)
    for h in range(B_HEADS):
        q = _seg_rms(p_ref[0, :, OFF_BQ + h * LANES:OFF_BQ + (h + 1) * LANES], ones128, B_HEAD_DIM)
        qb_ref[0, h] = (q * gbq * b_scale).astype(qb_ref.dtype)
    k = _seg_rms(p_ref[0, :, OFF_BK:OFF_BK + B_K], ones128, B_HEAD_DIM)
    kb_ref[0] = (k * gbk).astype(kb_ref.dtype)
    vb_ref[0] = p_ref[0, :, OFF_BV:OFF_BV + B_V].astype(vb_ref.dtype)
    for hp in range(IDX_HEADS // 2):
        qi = p_ref[0, :, OFF_IQ + hp * LANES:OFF_IQ + (hp + 1) * LANES] * i_scale
        qi2_ref[0, 2 * hp] = jnp.where(lo_half, qi, 0.0).astype(qi2_ref.dtype)
        qi2_ref[0, 2 * hp + 1] = jnp.where(lo_half, 0.0, qi).astype(qi2_ref.dtype)
    kt = p_ref[0, :, OFF_IK:OFF_IK + I_K]
    ki2_ref[0] = jnp.concatenate([kt, kt], axis=-1).astype(ki2_ref.dtype)


def _prep(proj3, gaq, gak, gbq, gbk):
    b, s, npad = proj3.shape
    ts = _pick(s, 256)
    dt = MXU_DTYPE
    head4 = lambda nh: jax.ShapeDtypeStruct((b, nh, s, LANES), dt)
    flat3 = jax.ShapeDtypeStruct((b, s, LANES), dt)
    hspec = lambda nh: pl.BlockSpec((1, nh, ts, LANES), lambda bi, i: (bi, 0, i, 0))
    fspec = pl.BlockSpec((1, ts, LANES), lambda bi, i: (bi, i, 0))
    gspec = pl.BlockSpec((1, LANES), lambda bi, i: (0, 0))
    return pl.pallas_call(
        _prep_kernel,
        out_shape=(jax.ShapeDtypeStruct((b, A_HEADS, 2, s, LANES), dt),
                   head4(A_HEADS), head4(A_HEADS), head4(B_HEADS), flat3, flat3,
                   head4(IDX_HEADS), flat3),
        grid=(b, s // ts),
        in_specs=[pl.BlockSpec((1, ts, npad), lambda bi, i: (bi, i, 0)), gspec, gspec, gspec, gspec],
        out_specs=(pl.BlockSpec((1, A_HEADS, 2, ts, LANES), lambda bi, i: (bi, 0, 0, i, 0)),
                   hspec(A_HEADS), hspec(A_HEADS), hspec(B_HEADS), fspec, fspec,
                   hspec(IDX_HEADS), fspec),
        compiler_params=_cparams("parallel", "parallel"),
        name="head_prep",
    )(proj3, gaq, gak, gbq, gbk)


def _rel_bucket_np(rel):
    nb = N_BUCKETS // 2
    max_exact = nb // 2
    bucket = np.where(rel > 0, nb, 0)
    n = np.abs(rel)
    nf = np.maximum(n, 1).astype(np.float32)
    large = max_exact + (np.log(nf / np.float32(max_exact)) / np.float32(math.log(MAX_DISTANCE / max_exact))
                         * np.float32(nb - max_exact)).astype(np.int32)
    large = np.minimum(large, nb - 1)
    return (bucket + np.where(n < max_exact, n, large)).astype(np.int32)


def _near_bucket_ids(t):
    tq = np.arange(t)[:, None]
    out = []
    for key0 in (0, -t):
        ts = key0 + np.arange(2 * t)[None, :]
        ids = _rel_bucket_np(ts - tq)
        vis = np.floor_divide(ts, CHUNK) <= np.floor_divide(tq, CHUNK)
        out.append(np.where(vis, ids, -1))
    return np.stack(out).astype(np.int32)


FAR_BUCKET = N_BUCKETS // 2 - 1


def _bias_kernel(tab_ref, ids_ref, o_ref):
    h = pl.program_id(0)
    ids = ids_ref[...]
    far = tab_ref[FAR_BUCKET, h]
    acc = jnp.full(ids.shape, NEG_INF, jnp.float32)
    for bkt in range(N_BUCKETS):
        acc = jnp.where(ids == bkt, (tab_ref[bkt, h] - far) * LOG2E, acc)
    o_ref[0] = acc


def _near_bias(rel_bias, t):
    ids = jnp.asarray(_near_bucket_ids(t))
    nh = rel_bias.shape[1]
    return pl.pallas_call(
        _bias_kernel,
        out_shape=jax.ShapeDtypeStruct((nh, 2, t, 2 * t), jnp.float32),
        grid=(nh,),
        in_specs=[pl.BlockSpec(memory_space=pltpu.SMEM),
                  pl.BlockSpec((2, t, 2 * t), lambda h: (0, 0, 0))],
        out_specs=pl.BlockSpec((1, 2, t, 2 * t), lambda h: (h, 0, 0, 0)),
        compiler_params=_cparams("arbitrary"),
        name="near_bias",
    )(rel_bias, ids)


def _softmax_step(s, v, m_sc, l_sc, acc_sc, idx, first):
    rows, width = s.shape
    blocks = [s[:, c * LANES:(c + 1) * LANES] for c in range(width // LANES)]
    m_part = functools.reduce(jnp.maximum, blocks)
    m_cur = jnp.broadcast_to(jnp.max(m_part, axis=-1, keepdims=True), (rows, LANES))
    if first:
        m_new = jnp.maximum(m_cur, M_FLOOR)
    else:
        m_prev = m_sc[idx]
        m_new = jnp.maximum(m_prev, m_cur)
        alpha = jnp.exp2(m_prev - m_new)
    ps = [jnp.exp2(blk - m_new) for blk in blocks]
    l_part = functools.reduce(jnp.add, ps)
    l_cur = jnp.broadcast_to(jnp.sum(l_part, axis=-1, keepdims=True), (rows, LANES))
    p = jnp.concatenate([x.astype(v.dtype) for x in ps], axis=-1)
    pv = _dot(p, v)
    if first:
        l_sc[idx] = l_cur
        acc_sc[idx] = pv
    else:
        l_sc[idx] = alpha * l_sc[idx] + l_cur
        acc_sc[idx] = alpha * acc_sc[idx] + pv
    m_sc[idx] = m_new


def _diff_kernel(q2_ref, k_ref, v_ref, bias_ref, lam_ref, g_ref, o_ref, m_sc, l_sc, acc_sc,
                 *, lam_init, t, hg):
    i = pl.program_id(2)
    n_far = jnp.maximum(i - 1, 0)
    near0 = pl.multiple_of(n_far * t, t)

    def step(k0, width, first, bias_col=None):
        for hh in range(hg):
            q2 = q2_ref[0, hh].reshape(2 * t, LANES)
            s = _dot_nt(q2, k_ref[0, hh, pl.ds(k0, width), :])
            if bias_col is not None:
                bias = bias_ref[hh, 0, :, bias_col:bias_col + width]
                s = (s.reshape(2, t, width) + bias[None]).reshape(2 * t, width)
            _softmax_step(s, v_ref[0, hh, pl.ds(k0, width), :], m_sc, l_sc, acc_sc, hh, first)

    step(near0, t, True, 0)
    step(near0 + t, t, False, t)

    def far_body(j, carry):
        step(pl.multiple_of(j * t, t), t, False)
        return carry

    lax.fori_loop(0, n_far, far_body, 0)

    lq = lam_ref[...]
    lam = (jnp.exp(jnp.sum(lq[0:1] * lq[1:2], axis=-1, keepdims=True))
           - jnp.exp(jnp.sum(lq[2:3] * lq[3:4], axis=-1, keepdims=True)) + lam_init)
    for hh in range(hg):
        o = acc_sc[hh] / l_sc[hh]
        o = o[:t] - lam * o[t:]
        ms = jnp.mean(o * o, axis=-1, keepdims=True)
        o = o * lax.rsqrt(ms + EPS) * g_ref[...]
        o_ref[:, hh * A_V_DIM:(hh + 1) * A_V_DIM] = (o * (1.0 - lam_init)).astype(o_ref.dtype)


def _diff_attention(qa2, ka, va, bias, lam_qk, g_out, lam_init):
    b, nh, _, s, _ = qa2.shape
    t = DIFF_T
    hg = DIFF_HEADS_PER_STEP
    nt = s // t
    rows = 2 * t
    return pl.pallas_call(
        functools.partial(_diff_kernel, lam_init=lam_init, t=t, hg=hg),
        out_shape=jax.ShapeDtypeStruct((b * s, nh * A_V_DIM), MXU_DTYPE),
        grid=(b, nh // hg, nt),
        in_specs=[pl.BlockSpec((1, hg, 2, t, LANES), lambda bi, h, i: (bi, h, 0, i, 0)),
                  pl.BlockSpec((1, hg, s, LANES), lambda bi, h, i: (bi, h, 0, 0)),
                  pl.BlockSpec((1, hg, s, LANES), lambda bi, h, i: (bi, h, 0, 0)),
                  pl.BlockSpec((hg, 1, t, 2 * t), lambda bi, h, i: (h, jnp.minimum(i, 1), 0, 0)),
                  pl.BlockSpec((4, A_QK_DIM), lambda bi, h, i: (0, 0)),
                  pl.BlockSpec((1, A_V_DIM), lambda bi, h, i: (0, 0))],
        out_specs=pl.BlockSpec((t, hg * A_V_DIM), lambda bi, h, i: (bi * nt + i, h)),
        scratch_shapes=[pltpu.VMEM((hg, rows, LANES), jnp.float32),
                        pltpu.VMEM((hg, rows, LANES), jnp.float32),
                        pltpu.VMEM((hg, rows, A_V_DIM), jnp.float32)],
        compiler_params=_cparams("parallel", "parallel", "arbitrary"),
        name="diff_attention",
    )(qa2, ka, va, bias, lam_qk, g_out.reshape(1, A_V_DIM))


def _sortable_key(x):
    bits = pltpu.bitcast(x, jnp.int32)
    return bits ^ ((bits >> 31) & 0x7FFFFFFF)


def _dsa_kernel(qi2_ref, ki2_ref, wi_ref, qb_ref, kb_ref, vb_ref, bias_ref, wo_ref, o_ref, woc_ref,
                key_sc, hi_sc, lo_sc, am_sc, m_sc, l_sc, acc_sc, *, topk, heads_per_group):
    woc_ref[...] = wo_ref[...].astype(woc_ref.dtype)
    t = ATT_T
    kb_sz = IDX_KB
    i = pl.program_id(1)
    q0 = i * t
    n_idx_blocks = (jnp.maximum(q0 + t, 2 * t) + kb_sz - 1) // kb_sz
    t_idx = q0 + lax.broadcasted_iota(jnp.int32, (1, t), 1)
    t_chunk = t_idx >> CHUNK_SHIFT

    wi = wi_ref[0] * (IDX_HEADS ** -0.5)

    def score_body(kbi, carry):
        k0 = pl.multiple_of(kbi * kb_sz, kb_sz)
        kblk = ki2_ref[0, pl.ds(k0, kb_sz), :]
        sc = jnp.zeros((kb_sz, t), jnp.float32)
        for h in range(IDX_HEADS):
            d = _dot_nt(kblk, qi2_ref[0, h])
            sc = sc + jnp.maximum(d, 0.0) * wi[h:h + 1, :]
        s_chunk = (k0 + lax.broadcasted_iota(jnp.int32, (kb_sz, 1), 0)) >> CHUNK_SHIFT
        sc = jnp.where(s_chunk <= t_chunk, sc, NEG_INF)
        key = _sortable_key(sc)
        key_sc[pl.ds(k0, kb_sz), :] = key
        hi_sc[pl.ds(k0, kb_sz), :] = (key >> 16).astype(jnp.int16)
        lo_sc[pl.ds(k0, kb_sz), :] = ((key & 0xFFFF) + I16_MIN).astype(jnp.int16)
        return carry

    lax.fori_loop(0, n_idx_blocks, score_body, 0)

    kk = jnp.minimum((t_chunk + 1) * CHUNK, topk)

    def count16(ref, cand, strict):
        cand16 = cand.astype(jnp.int16)

        def body(kbi, c):
            k0 = pl.multiple_of(kbi * kb_sz, kb_sz)
            blk = ref[pl.ds(k0, kb_sz), :]
            hit = (blk > cand16) if strict else (blk >= cand16)
            h3 = hit.astype(jnp.int16).reshape(kb_sz // 16, 16, t)
            for r in range(kb_sz // 16):
                c = c + h3[r]
            return c

        c16 = lax.fori_loop(0, n_idx_blocks, body, jnp.zeros((16, t), jnp.int16))
        return jnp.sum(c16.astype(jnp.int32), axis=0, keepdims=True)

    def search16(ref, base):
        def bit_body(it, thr):
            cand = thr + jnp.left_shift(jnp.int32(1), 15 - it)
            return jnp.where(base + count16(ref, cand, False) >= kk, cand, thr)
        return lax.fori_loop(0, 16, bit_body, jnp.full((1, t), I16_MIN, jnp.int32))

    thr_hi = search16(hi_sc, jnp.zeros((1, t), jnp.int32))
    above = count16(hi_sc, thr_hi, True)
    thr_hi16 = thr_hi.astype(jnp.int16)

    def tie_body(kbi, carry):
        k0 = pl.multiple_of(kbi * kb_sz, kb_sz)
        same = hi_sc[pl.ds(k0, kb_sz), :] == thr_hi16
        lo_sc[pl.ds(k0, kb_sz), :] = jnp.where(same, lo_sc[pl.ds(k0, kb_sz), :], jnp.int16(I16_MIN))
        return carry

    lax.fori_loop(0, n_idx_blocks, tie_body, 0)
    thr_lo = search16(lo_sc, above)
    thr = (thr_hi << 16) + (thr_lo - I16_MIN)
    n_greater = above + count16(lo_sc, thr_lo, True)
    slots = (kk - n_greater).astype(jnp.float32)

    tri = (lax.broadcasted_iota(jnp.int32, (kb_sz, kb_sz), 0)
           >= lax.broadcasted_iota(jnp.int32, (kb_sz, kb_sz), 1)).astype(MXU_DTYPE)

    def mask_body(kbi, ties_before):
        k0 = pl.multiple_of(kbi * kb_sz, kb_sz)
        key = key_sc[pl.ds(k0, kb_sz), :]
        tie = key == thr
        rank = ties_before + _dot(tri, tie.astype(MXU_DTYPE))
        sel = (key > thr) | (tie & (rank <= slots))
        am_sc[:, pl.ds(k0, kb_sz)] = jnp.where(sel, 0.0, NEG_INF).T
        return rank[kb_sz - 1:kb_sz, :]

    def plain_mask_body(kbi, carry):
        k0 = pl.multiple_of(kbi * kb_sz, kb_sz)
        am_sc[:, pl.ds(k0, kb_sz)] = jnp.where(key_sc[pl.ds(k0, kb_sz), :] >= thr, 0.0, NEG_INF).T
        return carry

    surplus = jnp.max(above + count16(lo_sc, thr_lo, False) - kk)

    @pl.when(surplus > 0)
    def _():
        lax.fori_loop(0, n_idx_blocks, mask_body, jnp.zeros((1, t), jnp.float32))

    @pl.when(surplus <= 0)
    def _():
        lax.fori_loop(0, n_idx_blocks, plain_mask_body, 0)

    g = heads_per_group
    n_groups = B_HEADS // g
    far_w = DSA_FAR_W
    near0 = pl.multiple_of(jnp.maximum(i - 1, 0) * t, t)
    am_near = am_sc[:, pl.ds(near0, 2 * t)]
    am_sc[:, pl.ds(near0, far_w + 2 * t)] = jnp.full((t, far_w + 2 * t), NEG_INF, jnp.float32)

    def step(k0, width, first, bias_col=None):
        kblk = kb_ref[0, pl.ds(k0, width), :]
        vblk = vb_ref[0, pl.ds(k0, width), :]
        for hg in range(n_groups):
            q = qb_ref[0, hg * g:(hg + 1) * g].reshape(g * t, LANES)
            s = _dot_nt(q, kblk).reshape(g, t, width)
            if bias_col is not None:
                cols = slice(bias_col, bias_col + width)
                s = s + (bias_ref[hg * g:(hg + 1) * g, 0, :, cols] + am_near[:, cols][None])
            else:
                s = s + am_sc[:, pl.ds(k0, width)][None]
            _softmax_step(s.reshape(g * t, width), vblk, m_sc, l_sc, acc_sc, hg, first)

    step(near0, t, True, 0)
    step(near0 + t, t, False, t)

    def far_body(j, carry):
        step(pl.multiple_of(j * far_w, far_w), far_w, False)
        return carry

    lax.fori_loop(0, (near0 + far_w - 1) // far_w, far_body, 0)

    for hg in range(n_groups):
        o = acc_sc[hg] / l_sc[hg]
        for hh in range(g):
            h = hg * g + hh
            o_ref[:, h * B_HEAD_DIM:(h + 1) * B_HEAD_DIM] = o[hh * t:(hh + 1) * t].astype(o_ref.dtype)


def _dsa_attention(qi2, ki2, wi_t, qb, kb, vb, bias, w_out_all, layer, topk):
    b, nh, s, _ = qb.shape
    t = ATT_T
    nt = s // t
    g = DSA_HEADS_PER_GROUP
    n_groups = nh // g
    nb_cols = B_HEADS * B_HEAD_DIM
    _, wo_rows, wo_cols = w_out_all.shape
    slab = wo_rows // (b * nt)
    assert slab * b * nt == wo_rows and slab % 16 == 0
    return pl.pallas_call(
        functools.partial(_dsa_kernel, topk=topk, heads_per_group=g),
        out_shape=(jax.ShapeDtypeStruct((b * s, nb_cols), MXU_DTYPE),
                   jax.ShapeDtypeStruct((wo_rows, wo_cols), MXU_DTYPE)),
        grid=(b, nt),
        in_specs=[pl.BlockSpec((1, IDX_HEADS, t, LANES), lambda bi, i: (bi, 0, i, 0)),
                  pl.BlockSpec((1, s, LANES), lambda bi, i: (bi, 0, 0)),
                  pl.BlockSpec((1, IDX_HEADS, t), lambda bi, i: (bi, 0, i)),
                  pl.BlockSpec((1, nh, t, LANES), lambda bi, i: (bi, 0, i, 0)),
                  pl.BlockSpec((1, s, LANES), lambda bi, i: (bi, 0, 0)),
                  pl.BlockSpec((1, s, LANES), lambda bi, i: (bi, 0, 0)),
                  pl.BlockSpec((B_HEADS, 1, t, 2 * t), lambda bi, i: (0, jnp.minimum(i, 1), 0, 0)),
                  pl.BlockSpec((None, slab, wo_cols), lambda bi, i: (layer, bi * nt + i, 0))],
        out_specs=(pl.BlockSpec((t, nb_cols), lambda bi, i: (bi * nt + i, 0)),
                   pl.BlockSpec((slab, wo_cols), lambda bi, i: (bi * nt + i, 0))),
        scratch_shapes=[pltpu.VMEM((s, t), jnp.int32),
                        pltpu.VMEM((s, t), jnp.int16),
                        pltpu.VMEM((s, t), jnp.int16),
                        pltpu.VMEM((t, s + DSA_FAR_W), jnp.float32),
                        pltpu.VMEM((n_groups, g * t, LANES), jnp.float32),
                        pltpu.VMEM((n_groups, g * t, LANES), jnp.float32),
                        pltpu.VMEM((n_groups, g * t, B_HEAD_DIM), jnp.float32)],
        compiler_params=_cparams("arbitrary", "arbitrary"),
        name="dsa_attention",
    )(qi2, ki2, wi_t, qb, kb, vb, bias, w_out_all)


def _ffn1_kernel(h_ref, ss_ref, hp_ref, ssp_ref, wg_ref, wu_ref, cw_ref, cb_ref, wd_ref,
                 o_ref, wdc_ref, *, tiles_per_seq):
    wdc_ref[...] = wd_ref[...].astype(wdc_ref.dtype)
    i = pl.program_id(0)
    tm, d = h_ref.shape
    h = h_ref[...]
    rs = _row_scale(ss_ref, d)
    wg = wg_ref[...].astype(MXU_DTYPE)
    gate = _dot(h, wg) * rs
    up = _dot(h, wu_ref[...].astype(MXU_DTYPE)) * (0.5 * rs)
    prev = _dot(hp_ref[...], wg) * _row_scale(ssp_ref, d)
    prev = jnp.where(i % tiles_per_seq == 0, 0.0, prev)
    cw = cw_ref[...]
    cb = cb_ref[...]

    def glu(g2, g1, g0, u):
        gc = cw[0:1] * g2 + cw[1:2] * g1 + cw[2:3] * g0 + cb
        return (gc * (1.0 + jnp.tanh(0.5 * gc)) * u).astype(o_ref.dtype)

    o_ref[...] = glu(pltpu.roll(gate, 2, 0), pltpu.roll(gate, 1, 0), gate, up)
    top = gate[0:8]
    row = lax.broadcasted_iota(jnp.int32, (8, 1), 0)
    t1 = jnp.where(row == 0, prev[7:8], pltpu.roll(top, 1, 0))
    t2 = jnp.where(row == 0, prev[6:7], jnp.where(row == 1, prev[7:8], pltpu.roll(top, 2, 0)))
    o_ref[0:8, :] = glu(t2, t1, top, up[0:8])


def _ffn1(h, ss, w_gu_all, conv_w_all, conv_b_all, w_down_all, layer, seq):
    m, d = h.shape
    dff = conv_w_all.shape[2]
    d_out = w_down_all.shape[2]
    tm = _pick(seq, 1024)
    tn = _pick(dff, 256)
    nj = dff // tn
    sub = 8
    n_steps = (m // tm) * nj
    slab = dff // n_steps
    assert slab * n_steps == dff and slab % 16 == 0
    return pl.pallas_call(
        functools.partial(_ffn1_kernel, tiles_per_seq=seq // tm),
        out_shape=(jax.ShapeDtypeStruct((m, dff), MXU_DTYPE),
                   jax.ShapeDtypeStruct((dff, d_out), MXU_DTYPE)),
        grid=(m // tm, nj),
        in_specs=[pl.BlockSpec((tm, d), lambda i, j: (i, 0)),
                  pl.BlockSpec((tm, LANES), lambda i, j: (i, 0)),
                  pl.BlockSpec((sub, d), lambda i, j: (jnp.maximum(i * (tm // sub) - 1, 0), 0)),
                  pl.BlockSpec((sub, LANES), lambda i, j: (jnp.maximum(i * (tm // sub) - 1, 0), 0)),
                  pl.BlockSpec((None, d, tn), lambda i, j: (layer, 0, j)),
                  pl.BlockSpec((None, d, tn), lambda i, j: (layer, 0, j + nj)),
                  pl.BlockSpec((None, CONV_WIDTH, tn), lambda i, j: (layer, 0, j)),
                  pl.BlockSpec((None, 1, tn), lambda i, j: (layer, 0, j)),
                  pl.BlockSpec((None, slab, d_out), lambda i, j: (layer, i * nj + j, 0))],
        out_specs=(pl.BlockSpec((tm, tn), lambda i, j: (i, j)),
                   pl.BlockSpec((slab, d_out), lambda i, j: (i * nj + j, 0))),
        compiler_params=_cparams("arbitrary", "arbitrary"),
        name="ffn_gate_up_glu",
    )(h, ss, h, ss, w_gu_all, w_gu_all, conv_w_all, conv_b_all.reshape(-1, 1, dff), w_down_all)


def kernel(x, attn_norm, w_in, a_q_norm, a_k_norm, lambda_qk, a_out_norm, b_q_norm, b_k_norm,
           rel_bias, w_out, ffn_norm, w_gate_up, conv_w, conv_b, w_down):
    bsz, s_len, d_model = x.shape
    depth = w_in.shape[0]
    m = bsz * s_len
    topk = min(TOPK_MAX, s_len // 4)
    assert s_len % (2 * DIFF_T) == 0 and s_len % DSA_FAR_W == 0 and s_len % IDX_KB == 0
    assert w_in.shape[2] == D_IN and w_out.shape[1] == A_V + B_Q

    near_bias_a = _near_bias(rel_bias[:, :A_HEADS], DIFF_T)
    near_bias_b = _near_bias(rel_bias[:, A_HEADS:], ATT_T)
    x2 = x.reshape(m, d_model)
    w_in_t = jnp.swapaxes(w_in, 1, 2)
    xg, ss = _norm_inputs(x2, attn_norm[0])
    for l in range(depth):
        lam_init = 0.8 - 0.6 * math.exp(-0.3 * l)
        proj = _matmul_nt(xg, ss, w_in_t, l, tm_pref=1024, tn_pref=512, name="in_proj")
        proj3 = proj.reshape(bsz, s_len, D_IN)
        two = lambda g: jnp.concatenate([g, g]).reshape(1, LANES)
        qa2, ka, va, qb, kb, vb, qi2, ki2 = _prep(
            proj3, two(a_q_norm[l]), two(a_k_norm[l]),
            b_q_norm[l].reshape(1, LANES), b_k_norm[l].reshape(1, LANES))
        wi_t = jnp.swapaxes(proj3[:, :, OFF_IW:OFF_IW + I_W], 1, 2)
        mix_a = _diff_attention(qa2, ka, va, near_bias_a, lambda_qk[l], a_out_norm[l], lam_init)
        mix_b, w_out_c = _dsa_attention(qi2, ki2, wi_t, qb, kb, vb, near_bias_b, w_out, l, topk)
        x2, xg, ss = _matmul2_res(mix_a, mix_b, w_out_c, x2, ffn_norm[l],
                                  tm_pref=1024, tn_pref=512, name="out_proj")
        act, w_down_c = _ffn1(xg, ss, w_gate_up, conv_w, conv_b, w_down, l, s_len)
        if l + 1 < depth:
            x2, xg, ss = _matmul_res(act, w_down_c, x2, attn_norm[l + 1],
                                     tm_pref=512, tn_pref=512, name="down_proj")
        else:
            x2 = _matmul_res(act, w_down_c, x2, tm_pref=512, tn_pref=512, name="down_proj")
    return x2.reshape(bsz, s_len, d_model)
```

```python
import functools
import math

import numpy as np
import jax
import jax.numpy as jnp
from jax import lax
from jax.experimental import pallas as pl
from jax.experimental.pallas import tpu as pltpu

CHUNK = 64
CHUNK_SHIFT = 6
A_HEADS = 16
A_QK_DIM = 64
A_V_DIM = 128
B_HEADS = 16
B_HEAD_DIM = 128
IDX_HEADS = 16
IDX_DIM = 64
TOPK_MAX = 256
N_BUCKETS = 32
MAX_DISTANCE = 128
CONV_WIDTH = 3
EPS = 1e-6

A_Q = A_HEADS * 2 * A_QK_DIM
A_K = A_HEADS * 2 * A_QK_DIM
A_V = A_HEADS * A_V_DIM
B_Q = B_HEADS * B_HEAD_DIM
B_K = B_HEAD_DIM
B_V = B_HEAD_DIM
I_Q = IDX_HEADS * IDX_DIM
I_K = IDX_DIM
I_W = IDX_HEADS
OFF_AQ = 0
OFF_AK = OFF_AQ + A_Q
OFF_AV = OFF_AK + A_K
OFF_BQ = OFF_AV + A_V
OFF_BK = OFF_BQ + B_Q
OFF_BV = OFF_BK + B_K
OFF_IQ = OFF_BV + B_V
OFF_IK = OFF_IQ + I_Q
OFF_IW = OFF_IK + I_K
D_IN = OFF_IW + I_W

LANES = 128
MXU_N = 256
V7X_VMEM_LIMIT = 58 * 1024 * 1024

DIFF_T = 256
DIFF_HEADS_PER_STEP = 8
ATT_T = 256
IDX_KB = 256
DSA_FAR_W = 256
DSA_HEADS_PER_GROUP = 2
MXU_DTYPE = jnp.bfloat16

LOG2E = math.log2(math.e)
NEG_INF = float("-inf")
M_FLOOR = -1e30
I16_MIN = -(2 ** 15)


def _cparams(*sem):
    return pltpu.CompilerParams(dimension_semantics=sem, vmem_limit_bytes=V7X_VMEM_LIMIT)


def _pick(n, pref):
    t = min(pref, n)
    while n % t:
        t -= LANES
    return t


def _dot(a, b):
    return jnp.dot(a, b, preferred_element_type=jnp.float32)


def _dot_nt(a, b):
    return lax.dot_general(a, b, (((1,), (1,)), ((), ())), preferred_element_type=jnp.float32)


def _row_sumsq(x):
    sq = functools.reduce(jnp.add, [x[:, c * LANES:(c + 1) * LANES] ** 2
                                    for c in range(x.shape[1] // LANES)])
    return jnp.broadcast_to(jnp.sum(sq, axis=-1, keepdims=True), (x.shape[0], LANES))


def _row_scale(ss_ref, d):
    return lax.rsqrt(ss_ref[:, 0:1] * (1.0 / d) + EPS)


def _col_chains(width):
    w = MXU_N if width % MXU_N == 0 else width
    return [slice(c, c + w) for c in range(0, width, w)]


def _accumulate_ss(ss_ref, j, parts):
    @pl.when(j == 0)
    def _():
        ss_ref[...] = jnp.zeros(ss_ref.shape, jnp.float32)

    ss_ref[...] += functools.reduce(jnp.add, parts)


def _norm_inputs_kernel(x_ref, g_ref, xg_ref, ss_ref):
    x = x_ref[...]
    xg_ref[...] = (x * g_ref[...]).astype(xg_ref.dtype)
    ss_ref[...] = _row_sumsq(x)


def _norm_inputs(x2d, g):
    m, d = x2d.shape
    tm = _pick(m, 256)
    return pl.pallas_call(
        _norm_inputs_kernel,
        out_shape=(jax.ShapeDtypeStruct((m, d), MXU_DTYPE),
                   jax.ShapeDtypeStruct((m, LANES), jnp.float32)),
        grid=(m // tm,),
        in_specs=[pl.BlockSpec((tm, d), lambda i: (i, 0)),
                  pl.BlockSpec((1, d), lambda i: (0, 0))],
        out_specs=(pl.BlockSpec((tm, d), lambda i: (i, 0)),
                   pl.BlockSpec((tm, LANES), lambda i: (i, 0))),
        compiler_params=_cparams("parallel"),
        name="norm_inputs",
    )(x2d, g.reshape(1, d))


def _mm_nt_kernel(a_ref, ss_ref, wt_ref, o_ref):
    a = a_ref[...]
    rs = _row_scale(ss_ref, a.shape[1])
    for cols in _col_chains(o_ref.shape[1]):
        acc = _dot_nt(a, wt_ref[cols, :].astype(MXU_DTYPE))
        o_ref[:, cols] = (acc * rs).astype(o_ref.dtype)


def _mm_res_kernel(a_ref, w_ref, r_ref, *rest, emit_norm):
    if emit_norm:
        g_ref, o_ref, xg_ref, ss_ref = rest
    else:
        (o_ref,) = rest
    a = a_ref[...]
    parts = []
    for cols in _col_chains(o_ref.shape[1]):
        x = r_ref[:, cols] + _dot(a, w_ref[:, cols])
        o_ref[:, cols] = x
        if emit_norm:
            xg_ref[:, cols] = (x * g_ref[:, cols]).astype(xg_ref.dtype)
            parts.append(_row_sumsq(x))
    if emit_norm:
        _accumulate_ss(ss_ref, pl.program_id(1), parts)


def _mm2_res_kernel(a1_ref, a2_ref, w1_ref, w2_ref, r_ref, g_ref, o_ref, xg_ref, ss_ref):
    a1, a2 = a1_ref[...], a2_ref[...]
    parts = []
    for cols in _col_chains(o_ref.shape[1]):
        acc = (_dot(a1, w1_ref[:, cols].astype(MXU_DTYPE))
               + _dot(a2, w2_ref[:, cols].astype(MXU_DTYPE)))
        x = r_ref[:, cols] + acc
        o_ref[:, cols] = x
        xg_ref[:, cols] = (x * g_ref[:, cols]).astype(xg_ref.dtype)
        parts.append(_row_sumsq(x))
    _accumulate_ss(ss_ref, pl.program_id(1), parts)


def _norm_out(m, n, tm, tn):
    shapes = (jax.ShapeDtypeStruct((m, n), jnp.float32),
              jax.ShapeDtypeStruct((m, n), MXU_DTYPE),
              jax.ShapeDtypeStruct((m, LANES), jnp.float32))
    specs = (pl.BlockSpec((tm, tn), lambda i, j: (i, j)),
             pl.BlockSpec((tm, tn), lambda i, j: (i, j)),
             pl.BlockSpec((tm, LANES), lambda i, j: (i, 0)))
    return shapes, specs


def _matmul2_res(a1, a2, w, res, g_next, *, tm_pref, tn_pref, name):
    m, k = a1.shape
    n = w.shape[1]
    tm, tn = _pick(m, tm_pref), _pick(n, tn_pref)
    shapes, specs = _norm_out(m, n, tm, tn)
    return pl.pallas_call(
        _mm2_res_kernel,
        out_shape=shapes,
        grid=(m // tm, n // tn),
        in_specs=[pl.BlockSpec((tm, k), lambda i, j: (i, 0)),
                  pl.BlockSpec((tm, k), lambda i, j: (i, 0)),
                  pl.BlockSpec((k, tn), lambda i, j: (0, j)),
                  pl.BlockSpec((k, tn), lambda i, j: (1, j)),
                  pl.BlockSpec((tm, tn), lambda i, j: (i, j)),
                  pl.BlockSpec((1, tn), lambda i, j: (0, j))],
        out_specs=specs,
        compiler_params=_cparams("parallel", "arbitrary"),
        name=name,
    )(a1, a2, w, w, res, g_next.reshape(1, n))


def _matmul_nt(a, ss, wt_all, layer, *, tm_pref, tn_pref, name):
    m, k = a.shape
    n = wt_all.shape[1]
    tm, tn = _pick(m, tm_pref), min(tn_pref, n)
    return pl.pallas_call(
        _mm_nt_kernel,
        out_shape=jax.ShapeDtypeStruct((m, n), jnp.float32),
        grid=(m // tm, pl.cdiv(n, tn)),
        in_specs=[pl.BlockSpec((tm, k), lambda i, j: (i, 0)),
                  pl.BlockSpec((tm, LANES), lambda i, j: (i, 0)),
                  pl.BlockSpec((None, tn, k), lambda i, j: (layer, j, 0))],
        out_specs=pl.BlockSpec((tm, tn), lambda i, j: (i, j)),
        compiler_params=_cparams("parallel", "arbitrary"),
        name=name,
    )(a, ss, wt_all)


def _matmul_res(a, w, res, g_next=None, *, tm_pref, tn_pref, name):
    m, k = a.shape
    n = w.shape[1]
    tm, tn = _pick(m, tm_pref), _pick(n, tn_pref)
    in_specs = [pl.BlockSpec((tm, k), lambda i, j: (i, 0)),
                pl.BlockSpec((k, tn), lambda i, j: (0, j)),
                pl.BlockSpec((tm, tn), lambda i, j: (i, j))]
    args = [a, w, res]
    if g_next is None:
        shapes = jax.ShapeDtypeStruct((m, n), jnp.float32)
        specs = pl.BlockSpec((tm, tn), lambda i, j: (i, j))
    else:
        shapes, specs = _norm_out(m, n, tm, tn)
        in_specs.append(pl.BlockSpec((1, tn), lambda i, j: (0, j)))
        args.append(g_next.reshape(1, n))
    return pl.pallas_call(
        functools.partial(_mm_res_kernel, emit_norm=g_next is not None),
        out_shape=shapes,
        grid=(m // tm, n // tn),
        in_specs=in_specs,
        out_specs=specs,
        compiler_params=_cparams("parallel", "arbitrary"),
        name=name,
    )(*args)


def _seg_ones(seg):
    r = lax.broadcasted_iota(jnp.int32, (LANES, LANES), 0) // seg
    c = lax.broadcasted_iota(jnp.int32, (LANES, LANES), 1) // seg
    return (r == c).astype(MXU_DTYPE)


def _seg_rms(x, ones, seg):
    sq = x * x
    hi = sq.astype(MXU_DTYPE)
    lo = (sq - hi.astype(jnp.float32)).astype(MXU_DTYPE)
    ss = _dot(hi, ones) + _dot(lo, ones)
    return x * lax.rsqrt(ss * (1.0 / seg) + EPS)


def _prep_kernel(p_ref, gaq_ref, gak_ref, gbq_ref, gbk_ref,
                 qa_ref, ka_ref, va_ref, qb_ref, kb_ref, vb_ref, qi2_ref, ki2_ref):
    ones64 = _seg_ones(A_QK_DIM)
    ones128 = _seg_ones(B_HEAD_DIM)
    lane = lax.broadcasted_iota(jnp.int32, (1, LANES), 1)
    lo_half = lane < A_QK_DIM
    gaq, gak, gbq, gbk = gaq_ref[...], gak_ref[...], gbq_ref[...], gbk_ref[...]
    a_scale = A_QK_DIM ** -0.5 * LOG2E
    b_scale = B_HEAD_DIM ** -0.5 * LOG2E
    i_scale = IDX_DIM ** -0.5

    for h in range(A_HEADS):
        q = _seg_rms(p_ref[0, :, OFF_AQ + h * LANES:OFF_AQ + (h + 1) * LANES], ones64, A_QK_DIM)
        qa_ref[0, h] = (q * gaq * a_scale).astype(qa_ref.dtype)
        k = _seg_rms(p_ref[0, :, OFF_AK + h * LANES:OFF_AK + (h + 1) * LANES], ones64, A_QK_DIM)
        ka_ref[0, h] = (k * gak).astype(ka_ref.dtype)
        va_ref[0, h] = p_ref[0, :, OFF_AV + h * LANES:OFF_AV + (h + 1) * LANES].astype(va_ref.dtype)
    for h in range(B_HEADS):
        q = _seg_rms(p_ref[0, :, OFF_BQ + h * LANES:OFF_BQ + (h + 1) * LANES], ones128, B_HEAD_DIM)
        qb_ref[0, h] = (q * gbq * b_scale).astype(qb_ref.dtype)
    k = _seg_rms(p_ref[0, :, OFF_BK:OFF_BK + B_K], ones128, B_HEAD_DIM)
    kb_ref[0] = (k * gbk).astype(kb_ref.dtype)
    vb_ref[0] = p_ref[0, :, OFF_BV:OFF_BV + B_V].astype(vb_ref.dtype)
    for hp in range(IDX_HEADS // 2):
        qi = p_ref[0, :, OFF_IQ + hp * LANES:OFF_IQ + (hp + 1) * LANES] * i_scale
        qi2_ref[0, 2 * hp] = jnp.where(lo_half, qi, 0.0).astype(qi2_ref.dtype)
        qi2_ref[0, 2 * hp + 1] = jnp.where(lo_half, 0.0, qi).astype(qi2_ref.dtype)
    kt = p_ref[0, :, OFF_IK:OFF_IK + I_K]
    ki2_ref[0] = jnp.concatenate([kt, kt], axis=-1).astype(ki2_ref.dtype)


def _prep(proj3, gaq, gak, gbq, gbk):
    b, s, npad = proj3.shape
    ts = _pick(s, 256)
    dt = MXU_DTYPE
    head4 = lambda nh: jax.ShapeDtypeStruct((b, nh, s, LANES), dt)
    flat3 = jax.ShapeDtypeStruct((b, s, LANES), dt)
    hspec = lambda nh: pl.BlockSpec((1, nh, ts, LANES), lambda bi, i: (bi, 0, i, 0))
    fspec = pl.BlockSpec((1, ts, LANES), lambda bi, i: (bi, i, 0))
    gspec = pl.BlockSpec((1, LANES), lambda bi, i: (0, 0))
    return pl.pallas_call(
        _prep_kernel,
        out_shape=(head4(A_HEADS), head4(A_HEADS), head4(A_HEADS), head4(B_HEADS), flat3, flat3,
                   head4(IDX_HEADS), flat3),
        grid=(b, s // ts),
        in_specs=[pl.BlockSpec((1, ts, npad), lambda bi, i: (bi, i, 0)), gspec, gspec, gspec, gspec],
        out_specs=(hspec(A_HEADS), hspec(A_HEADS), hspec(A_HEADS), hspec(B_HEADS), fspec, fspec,
                   hspec(IDX_HEADS), fspec),
        compiler_params=_cparams("parallel", "parallel"),
        name="head_prep",
    )(proj3, gaq, gak, gbq, gbk)


def _rel_bucket_np(rel):
    nb = N_BUCKETS // 2
    max_exact = nb // 2
    bucket = np.where(rel > 0, nb, 0)
    n = np.abs(rel)
    nf = np.maximum(n, 1).astype(np.float32)
    large = max_exact + (np.log(nf / np.float32(max_exact)) / np.float32(math.log(MAX_DISTANCE / max_exact))
                         * np.float32(nb - max_exact)).astype(np.int32)
    large = np.minimum(large, nb - 1)
    return (bucket + np.where(n < max_exact, n, large)).astype(np.int32)


def _near_bucket_ids(t):
    tq = np.arange(t)[:, None]
    out = []
    for key0 in (0, -t):
        ts = key0 + np.arange(2 * t)[None, :]
        ids = _rel_bucket_np(ts - tq)
        vis = np.floor_divide(ts, CHUNK) <= np.floor_divide(tq, CHUNK)
        out.append(np.where(vis, ids, -1))
    return np.stack(out).astype(np.int32)


FAR_BUCKET = N_BUCKETS // 2 - 1


def _bias_kernel(tab_ref, ids_ref, o_ref):
    h = pl.program_id(0)
    ids = ids_ref[...]
    far = tab_ref[FAR_BUCKET, h]
    acc = jnp.full(ids.shape, NEG_INF, jnp.float32)
    for bkt in range(N_BUCKETS):
        acc = jnp.where(ids == bkt, (tab_ref[bkt, h] - far) * LOG2E, acc)
    o_ref[0] = acc


def _near_bias(rel_bias, t):
    ids = jnp.asarray(_near_bucket_ids(t))
    nh = rel_bias.shape[1]
    return pl.pallas_call(
        _bias_kernel,
        out_shape=jax.ShapeDtypeStruct((nh, 2, t, 2 * t), jnp.float32),
        grid=(nh,),
        in_specs=[pl.BlockSpec(memory_space=pltpu.SMEM),
                  pl.BlockSpec((2, t, 2 * t), lambda h: (0, 0, 0))],
        out_specs=pl.BlockSpec((1, 2, t, 2 * t), lambda h: (h, 0, 0, 0)),
        compiler_params=_cparams("arbitrary"),
        name="near_bias",
    )(rel_bias, ids)


def _softmax_step(s, v, m_sc, l_sc, acc_sc, idx, first):
    rows, width = s.shape
    blocks = [s[:, c * LANES:(c + 1) * LANES] for c in range(width // LANES)]
    m_part = functools.reduce(jnp.maximum, blocks)
    m_cur = jnp.broadcast_to(jnp.max(m_part, axis=-1, keepdims=True), (rows, LANES))
    if first:
        m_new = jnp.maximum(m_cur, M_FLOOR)
    else:
        m_prev = m_sc[idx]
        m_new = jnp.maximum(m_prev, m_cur)
        alpha = jnp.exp2(m_prev - m_new)
    ps = [jnp.exp2(blk - m_new) for blk in blocks]
    l_part = functools.reduce(jnp.add, ps)
    l_cur = jnp.broadcast_to(jnp.sum(l_part, axis=-1, keepdims=True), (rows, LANES))
    p = jnp.concatenate([x.astype(v.dtype) for x in ps], axis=-1)
    pv = _dot(p, v)
    if first:
        l_sc[idx] = l_cur
        acc_sc[idx] = pv
    else:
        l_sc[idx] = alpha * l_sc[idx] + l_cur
        acc_sc[idx] = alpha * acc_sc[idx] + pv
    m_sc[idx] = m_new


def _diff_kernel(q_ref, k_ref, v_ref, bias_ref, lam_ref, g_ref, o_ref, q2_sc, m_sc, l_sc, acc_sc,
                 *, lam_init, t, hg):
    i = pl.program_id(2)
    n_far = jnp.maximum(i - 1, 0)
    near0 = pl.multiple_of(n_far * t, t)

    lo_half = lax.broadcasted_iota(jnp.int32, (1, LANES), 1) < A_QK_DIM
    for hh in range(hg):
        q = q_ref[0, hh]
        zero = jnp.zeros_like(q)
        q2_sc[hh, 0:t] = jnp.where(lo_half, q, zero)
        q2_sc[hh, t:2 * t] = jnp.where(lo_half, zero, q)

    def step(k0, width, first, bias_col=None):
        for hh in range(hg):
            s = _dot_nt(q2_sc[hh], k_ref[0, hh, pl.ds(k0, width), :])
            if bias_col is not None:
                bias = bias_ref[hh, 0, :, bias_col:bias_col + width]
                s = (s.reshape(2, t, width) + bias[None]).reshape(2 * t, width)
            _softmax_step(s, v_ref[0, hh, pl.ds(k0, width), :], m_sc, l_sc, acc_sc, hh, first)

    step(near0, t, True, 0)
    step(near0 + t, t, False, t)

    def far_body(j, carry):
        step(pl.multiple_of(j * t, t), t, False)
        return carry

    lax.fori_loop(0, n_far, far_body, 0)

    lq = lam_ref[...]
    lam = (jnp.exp(jnp.sum(lq[0:1] * lq[1:2], axis=-1, keepdims=True))
           - jnp.exp(jnp.sum(lq[2:3] * lq[3:4], axis=-1, keepdims=True)) + lam_init)
    for hh in range(hg):
        o = acc_sc[hh] / l_sc[hh]
        o = o[:t] - lam * o[t:]
        ms = jnp.mean(o * o, axis=-1, keepdims=True)
        o = o * lax.rsqrt(ms + EPS) * g_ref[...]
        o_ref[:, hh * A_V_DIM:(hh + 1) * A_V_DIM] = (o * (1.0 - lam_init)).astype(o_ref.dtype)


def _diff_attention(qa, ka, va, bias, lam_qk, g_out, lam_init):
    b, nh, s, _ = qa.shape
    t = DIFF_T
    hg = DIFF_HEADS_PER_STEP
    nt = s // t
    rows = 2 * t
    return pl.pallas_call(
        functools.partial(_diff_kernel, lam_init=lam_init, t=t, hg=hg),
        out_shape=jax.ShapeDtypeStruct((b * s, nh * A_V_DIM), MXU_DTYPE),
        grid=(b, nh // hg, nt),
        in_specs=[pl.BlockSpec((1, hg, t, LANES), lambda bi, h, i: (bi, h, i, 0)),
                  pl.BlockSpec((1, hg, s, LANES), lambda bi, h, i: (bi, h, 0, 0)),
                  pl.BlockSpec((1, hg, s, LANES), lambda bi, h, i: (bi, h, 0, 0)),
                  pl.BlockSpec((hg, 1, t, 2 * t), lambda bi, h, i: (h, jnp.minimum(i, 1), 0, 0)),
                  pl.BlockSpec((4, A_QK_DIM), lambda bi, h, i: (0, 0)),
                  pl.BlockSpec((1, A_V_DIM), lambda bi, h, i: (0, 0))],
        out_specs=pl.BlockSpec((t, hg * A_V_DIM), lambda bi, h, i: (bi * nt + i, h)),
        scratch_shapes=[pltpu.VMEM((hg, rows, LANES), MXU_DTYPE),
                        pltpu.VMEM((hg, rows, LANES), jnp.float32),
                        pltpu.VMEM((hg, rows, LANES), jnp.float32),
                        pltpu.VMEM((hg, rows, A_V_DIM), jnp.float32)],
        compiler_params=_cparams("parallel", "parallel", "arbitrary"),
        name="diff_attention",
    )(qa, ka, va, bias, lam_qk, g_out.reshape(1, A_V_DIM))


def _sortable_key(x):
    bits = pltpu.bitcast(x, jnp.int32)
    return bits ^ ((bits >> 31) & 0x7FFFFFFF)


def _dsa_kernel(qi2_ref, ki2_ref, wi_ref, qb_ref, kb_ref, vb_ref, bias_ref, wo_ref, o_ref, woc_ref,
                key_sc, hi_sc, lo_sc, am_sc, m_sc, l_sc, acc_sc, *, topk, heads_per_group):
    woc_ref[...] = wo_ref[...].astype(woc_ref.dtype)
    t = ATT_T
    kb_sz = IDX_KB
    i = pl.program_id(1)
    q0 = i * t
    n_idx_blocks = (jnp.maximum(q0 + t, 2 * t) + kb_sz - 1) // kb_sz
    t_idx = q0 + lax.broadcasted_iota(jnp.int32, (1, t), 1)
    t_chunk = t_idx >> CHUNK_SHIFT

    wi = wi_ref[0] * (IDX_HEADS ** -0.5)

    def score_body(kbi, carry):
        k0 = pl.multiple_of(kbi * kb_sz, kb_sz)
        kblk = ki2_ref[0, pl.ds(k0, kb_sz), :]
        sc = jnp.zeros((kb_sz, t), jnp.float32)
        for h in range(IDX_HEADS):
            d = _dot_nt(kblk, qi2_ref[0, h])
            sc = sc + jnp.maximum(d, 0.0) * wi[h:h + 1, :]
        s_chunk = (k0 + lax.broadcasted_iota(jnp.int32, (kb_sz, 1), 0)) >> CHUNK_SHIFT
        sc = jnp.where(s_chunk <= t_chunk, sc, NEG_INF)
        key = _sortable_key(sc)
        key_sc[pl.ds(k0, kb_sz), :] = key
        hi_sc[pl.ds(k0, kb_sz), :] = (key >> 16).astype(jnp.int16)
        lo_sc[pl.ds(k0, kb_sz), :] = ((key & 0xFFFF) + I16_MIN).astype(jnp.int16)
        return carry

    lax.fori_loop(0, n_idx_blocks, score_body, 0)

    kk = jnp.minimum((t_chunk + 1) * CHUNK, topk)

    def count16(ref, cand, strict):
        cand16 = cand.astype(jnp.int16)

        def body(kbi, c):
            k0 = pl.multiple_of(kbi * kb_sz, kb_sz)
            blk = ref[pl.ds(k0, kb_sz), :]
            hit = (blk > cand16) if strict else (blk >= cand16)
            h3 = hit.astype(jnp.int16).reshape(kb_sz // 16, 16, t)
            for r in range(kb_sz // 16):
                c = c + h3[r]
            return c

        c16 = lax.fori_loop(0, n_idx_blocks, body, jnp.zeros((16, t), jnp.int16))
        return jnp.sum(c16.astype(jnp.int32), axis=0, keepdims=True)

    def search16(ref, base):
        def bit_body(it, thr):
            cand = thr + jnp.left_shift(jnp.int32(1), 15 - it)
            return jnp.where(base + count16(ref, cand, False) >= kk, cand, thr)
        return lax.fori_loop(0, 16, bit_body, jnp.full((1, t), I16_MIN, jnp.int32))

    thr_hi = search16(hi_sc, jnp.zeros((1, t), jnp.int32))
    above = count16(hi_sc, thr_hi, True)
    thr_hi16 = thr_hi.astype(jnp.int16)

    def tie_body(kbi, carry):
        k0 = pl.multiple_of(kbi * kb_sz, kb_sz)
        same = hi_sc[pl.ds(k0, kb_sz), :] == thr_hi16
        lo_sc[pl.ds(k0, kb_sz), :] = jnp.where(same, lo_sc[pl.ds(k0, kb_sz), :], jnp.int16(I16_MIN))
        return carry

    lax.fori_loop(0, n_idx_blocks, tie_body, 0)
    thr_lo = search16(lo_sc, above)
    thr = (thr_hi << 16) + (thr_lo - I16_MIN)
    n_greater = above + count16(lo_sc, thr_lo, True)
    slots = (kk - n_greater).astype(jnp.float32)

    tri = (lax.broadcasted_iota(jnp.int32, (kb_sz, kb_sz), 0)
           >= lax.broadcasted_iota(jnp.int32, (kb_sz, kb_sz), 1)).astype(MXU_DTYPE)

    def mask_body(kbi, ties_before):
        k0 = pl.multiple_of(kbi * kb_sz, kb_sz)
        key = key_sc[pl.ds(k0, kb_sz), :]
        tie = key == thr
        rank = ties_before + _dot(tri, tie.astype(MXU_DTYPE))
        sel = (key > thr) | (tie & (rank <= slots))
        am_sc[:, pl.ds(k0, kb_sz)] = jnp.where(sel, 0.0, NEG_INF).T
        return rank[kb_sz - 1:kb_sz, :]

    def plain_mask_body(kbi, carry):
        k0 = pl.multiple_of(kbi * kb_sz, kb_sz)
        am_sc[:, pl.ds(k0, kb_sz)] = jnp.where(key_sc[pl.ds(k0, kb_sz), :] >= thr, 0.0, NEG_INF).T
        return carry

    surplus = jnp.max(above + count16(lo_sc, thr_lo, False) - kk)

    @pl.when(surplus > 0)
    def _():
        lax.fori_loop(0, n_idx_blocks, mask_body, jnp.zeros((1, t), jnp.float32))

    @pl.when(surplus <= 0)
    def _():
        lax.fori_loop(0, n_idx_blocks, plain_mask_body, 0)

    g = heads_per_group
    n_groups = B_HEADS // g
    far_w = DSA_FAR_W
    near0 = pl.multiple_of(jnp.maximum(i - 1, 0) * t, t)
    am_near = am_sc[:, pl.ds(near0, 2 * t)]
    am_sc[:, pl.ds(near0, far_w + 2 * t)] = jnp.full((t, far_w + 2 * t), NEG_INF, jnp.float32)

    def step(k0, width, first, bias_col=None):
        kblk = kb_ref[0, pl.ds(k0, width), :]
        vblk = vb_ref[0, pl.ds(k0, width), :]
        for hg in range(n_groups):
            q = qb_ref[0, hg * g:(hg + 1) * g].reshape(g * t, LANES)
            s = _dot_nt(q, kblk).reshape(g, t, width)
            if bias_col is not None:
                cols = slice(bias_col, bias_col + width)
                s = s + (bias_ref[hg * g:(hg + 1) * g, 0, :, cols] + am_near[:, cols][None])
            else:
                s = s + am_sc[:, pl.ds(k0, width)][None]
            _softmax_step(s.reshape(g * t, width), vblk, m_sc, l_sc, acc_sc, hg, first)

    step(near0, t, True, 0)
    step(near0 + t, t, False, t)

    def far_body(j, carry):
        step(pl.multiple_of(j * far_w, far_w), far_w, False)
        return carry

    lax.fori_loop(0, (near0 + far_w - 1) // far_w, far_body, 0)

    for hg in range(n_groups):
        o = acc_sc[hg] / l_sc[hg]
        for hh in range(g):
            h = hg * g + hh
            o_ref[:, h * B_HEAD_DIM:(h + 1) * B_HEAD_DIM] = o[hh * t:(hh + 1) * t].astype(o_ref.dtype)


def _dsa_attention(qi2, ki2, wi_t, qb, kb, vb, bias, w_out_all, layer, topk):
    b, nh, s, _ = qb.shape
    t = ATT_T
    nt = s // t
    g = DSA_HEADS_PER_GROUP
    n_groups = nh // g
    nb_cols = B_HEADS * B_HEAD_DIM
    _, wo_rows, wo_cols = w_out_all.shape
    slab = wo_rows // (b * nt)
    assert slab * b * nt == wo_rows and slab % 16 == 0
    return pl.pallas_call(
        functools.partial(_dsa_kernel, topk=topk, heads_per_group=g),
        out_shape=(jax.ShapeDtypeStruct((b * s, nb_cols), MXU_DTYPE),
                   jax.ShapeDtypeStruct((wo_rows, wo_cols), MXU_DTYPE)),
        grid=(b, nt),
        in_specs=[pl.BlockSpec((1, IDX_HEADS, t, LANES), lambda bi, i: (bi, 0, i, 0)),
                  pl.BlockSpec((1, s, LANES), lambda bi, i: (bi, 0, 0)),
                  pl.BlockSpec((1, IDX_HEADS, t), lambda bi, i: (bi, 0, i)),
                  pl.BlockSpec((1, nh, t, LANES), lambda bi, i: (bi, 0, i, 0)),
                  pl.BlockSpec((1, s, LANES), lambda bi, i: (bi, 0, 0)),
                  pl.BlockSpec((1, s, LANES), lambda bi, i: (bi, 0, 0)),
                  pl.BlockSpec((B_HEADS, 1, t, 2 * t), lambda bi, i: (0, jnp.minimum(i, 1), 0, 0)),
                  pl.BlockSpec((None, slab, wo_cols), lambda bi, i: (layer, bi * nt + i, 0))],
        out_specs=(pl.BlockSpec((t, nb_cols), lambda bi, i: (bi * nt + i, 0)),
                   pl.BlockSpec((slab, wo_cols), lambda bi, i: (bi * nt + i, 0))),
        scratch_shapes=[pltpu.VMEM((s, t), jnp.int32),
                        pltpu.VMEM((s, t), jnp.int16),
                        pltpu.VMEM((s, t), jnp.int16),
                        pltpu.VMEM((t, s + DSA_FAR_W), jnp.float32),
                        pltpu.VMEM((n_groups, g * t, LANES), jnp.float32),
                        pltpu.VMEM((n_groups, g * t, LANES), jnp.float32),
                        pltpu.VMEM((n_groups, g * t, B_HEAD_DIM), jnp.float32)],
        compiler_params=_cparams("arbitrary", "arbitrary"),
        name="dsa_attention",
    )(qi2, ki2, wi_t, qb, kb, vb, bias, w_out_all)


def _ffn1_kernel(h_ref, ss_ref, hp_ref, ssp_ref, wg_ref, wu_ref, cw_ref, cb_ref, wd_ref,
                 o_ref, wdc_ref, *, tiles_per_seq):
    wdc_ref[...] = wd_ref[...].astype(wdc_ref.dtype)
    i = pl.program_id(0)
    tm, d = h_ref.shape
    h = h_ref[...]
    rs = _row_scale(ss_ref, d)
    wg = wg_ref[...].astype(MXU_DTYPE)
    gate = _dot(h, wg) * rs
    up = _dot(h, wu_ref[...].astype(MXU_DTYPE)) * (0.5 * rs)
    prev = _dot(hp_ref[...], wg) * _row_scale(ssp_ref, d)
    prev = jnp.where(i % tiles_per_seq == 0, 0.0, prev)
    cw = cw_ref[...]
    cb = cb_ref[...]

    def glu(g2, g1, g0, u):
        gc = cw[0:1] * g2 + cw[1:2] * g1 + cw[2:3] * g0 + cb
        return (gc * (1.0 + jnp.tanh(0.5 * gc)) * u).astype(o_ref.dtype)

    o_ref[...] = glu(pltpu.roll(gate, 2, 0), pltpu.roll(gate, 1, 0), gate, up)
    top = gate[0:8]
    row = lax.broadcasted_iota(jnp.int32, (8, 1), 0)
    t1 = jnp.where(row == 0, prev[7:8], pltpu.roll(top, 1, 0))
    t2 = jnp.where(row == 0, prev[6:7], jnp.where(row == 1, prev[7:8], pltpu.roll(top, 2, 0)))
    o_ref[0:8, :] = glu(t2, t1, top, up[0:8])


def _ffn1(h, ss, w_gu_all, conv_w_all, conv_b_all, w_down_all, layer, seq):
    m, d = h.shape
    dff = conv_w_all.shape[2]
    d_out = w_down_all.shape[2]
    tm = _pick(seq, 1024)
    tn = _pick(dff, 256)
    nj = dff // tn
    sub = 8
    n_steps = (m // tm) * nj
    slab = dff // n_steps
    assert slab * n_steps == dff and slab % 16 == 0
    return pl.pallas_call(
        functools.partial(_ffn1_kernel, tiles_per_seq=seq // tm),
        out_shape=(jax.ShapeDtypeStruct((m, dff), MXU_DTYPE),
                   jax.ShapeDtypeStruct((dff, d_out), MXU_DTYPE)),
        grid=(m // tm, nj),
        in_specs=[pl.BlockSpec((tm, d), lambda i, j: (i, 0)),
                  pl.BlockSpec((tm, LANES), lambda i, j: (i, 0)),
                  pl.BlockSpec((sub, d), lambda i, j: (jnp.maximum(i * (tm // sub) - 1, 0), 0)),
                  pl.BlockSpec((sub, LANES), lambda i, j: (jnp.maximum(i * (tm // sub) - 1, 0), 0)),
                  pl.BlockSpec((None, d, tn), lambda i, j: (layer, 0, j)),
                  pl.BlockSpec((None, d, tn), lambda i, j: (layer, 0, j + nj)),
                  pl.BlockSpec((None, CONV_WIDTH, tn), lambda i, j: (layer, 0, j)),
                  pl.BlockSpec((None, 1, tn), lambda i, j: (layer, 0, j)),
                  pl.BlockSpec((None, slab, d_out), lambda i, j: (layer, i * nj + j, 0))],
        out_specs=(pl.BlockSpec((tm, tn), lambda i, j: (i, j)),
                   pl.BlockSpec((slab, d_out), lambda i, j: (i * nj + j, 0))),
        compiler_params=_cparams("arbitrary", "arbitrary"),
        name="ffn_gate_up_glu",
    )(h, ss, h, ss, w_gu_all, w_gu_all, conv_w_all, conv_b_all.reshape(-1, 1, dff), w_down_all)


def kernel(x, attn_norm, w_in, a_q_norm, a_k_norm, lambda_qk, a_out_norm, b_q_norm, b_k_norm,
           rel_bias, w_out, ffn_norm, w_gate_up, conv_w, conv_b, w_down):
    bsz, s_len, d_model = x.shape
    depth = w_in.shape[0]
    m = bsz * s_len
    topk = min(TOPK_MAX, s_len // 4)
    assert s_len % (2 * DIFF_T) == 0 and s_len % DSA_FAR_W == 0 and s_len % IDX_KB == 0
    assert w_in.shape[2] == D_IN and w_out.shape[1] == A_V + B_Q

    near_bias_a = _near_bias(rel_bias[:, :A_HEADS], DIFF_T)
    near_bias_b = _near_bias(rel_bias[:, A_HEADS:], ATT_T)
    x2 = x.reshape(m, d_model)
    w_in_t = jnp.swapaxes(w_in, 1, 2)
    xg, ss = _norm_inputs(x2, attn_norm[0])
    for l in range(depth):
        lam_init = 0.8 - 0.6 * math.exp(-0.3 * l)
        proj = _matmul_nt(xg, ss, w_in_t, l, tm_pref=1024, tn_pref=512, name="in_proj")
        proj3 = proj.reshape(bsz, s_len, D_IN)
        two = lambda g: jnp.concatenate([g, g]).reshape(1, LANES)
        qa, ka, va, qb, kb, vb, qi2, ki2 = _prep(
            proj3, two(a_q_norm[l]), two(a_k_norm[l]),
            b_q_norm[l].reshape(1, LANES), b_k_norm[l].reshape(1, LANES))
        wi_t = jnp.swapaxes(proj3[:, :, OFF_IW:OFF_IW + I_W], 1, 2)
        mix_a = _diff_attention(qa, ka, va, near_bias_a, lambda_qk[l], a_out_norm[l], lam_init)
        mix_b, w_out_c = _dsa_attention(qi2, ki2, wi_t, qb, kb, vb, near_bias_b, w_out, l, topk)
        x2, xg, ss = _matmul2_res(mix_a, mix_b, w_out_c, x2, ffn_norm[l],
                                  tm_pref=1024, tn_pref=512, name="out_proj")
        act, w_down_c = _ffn1(xg, ss, w_gate_up, conv_w, conv_b, w_down, l, s_len)
        if l + 1 < depth:
            x2, xg, ss = _matmul_res(act, w_down_c, x2, attn_norm[l + 1],
                                     tm_pref=512, tn_pref=512, name="down_proj")
        else:
            x2 = _matmul_res(act, w_down_c, x2, tm_pref=512, tn_pref=512, name="down_proj")
    return x2.reshape(bsz, s_len, d_model)
```
